```python
import jax, jax.numpy as jnp
from jax import lax
import numpy as np

D_MODEL = 1024
BATCH = 16
SEQ = 2048
DEPTH = 1

CONV_WIDTH = 1024
CONV_K = 3
N_HEADS = 8
HEAD_DIM = 64
ATTN_WIDTH = N_HEADS * HEAD_DIM
MOBA_BLOCK = 256
MOBA_TOPK = 3
Q_BLOCK = 128
ROPE_THETA = 500000.0
ROPE_DIM = HEAD_DIM // 4
N_BRANCHES = 2
IN_COLS = 3 * CONV_WIDTH + 3 * ATTN_WIDTH + N_BRANCHES * D_MODEL
N_EXPERTS = 256
TOP_K = 8
N_GROUPS = 8
TOPK_GROUPS = 4
EXPERT_DIM = 256
SHARED_DIM = 256
ROUTED_SCALE = 2.5
MOE_BLOCK = 128
MOE_CHUNK = 8
DN_ALPHA = (2 * DEPTH) ** 0.25
DN_BETA = (8 * DEPTH) ** -0.25
LN_EPS = 1e-5

kernel_name = "hybrid_conv_moba_moe_deepnorm"


def layer_norm(x, g, b):
    xf = x.astype(jnp.float32)
    mu = jnp.mean(xf, -1, keepdims=True)
    var = jnp.mean(jnp.square(xf - mu), -1, keepdims=True)
    return ((xf - mu) * lax.rsqrt(var + LN_EPS)).astype(x.dtype) * g + b


def partial_rope(t, pos):
    half = ROPE_DIM // 2
    inv = jnp.power(ROPE_THETA, -jnp.arange(half, dtype=jnp.float32) * 2.0 / ROPE_DIM)
    ang = pos.astype(jnp.float32)[:, None] * inv[None, :]
    cos = jnp.cos(ang).astype(t.dtype)
    sin = jnp.sin(ang).astype(t.dtype)
    t1 = t[..., :half]
    t2 = t[..., half:ROPE_DIM]
    return jnp.concatenate([t1 * cos - t2 * sin, t2 * cos + t1 * sin, t[..., ROPE_DIM:]], axis=-1)


def short_conv_mixer(b_gate, c_gate, v, w_conv):
    u = c_gate * v
    y = lax.conv_general_dilated(
        u, w_conv[:, None, :].astype(u.dtype), window_strides=(1,),
        padding=[(CONV_K - 1, 0)], dimension_numbers=("NWC", "WIO", "NWC"),
        feature_group_count=CONV_WIDTH)
    return b_gate * y


def moba_attention(q, k, v):
    b_, h_, s_, dh = q.shape
    nb = -(-s_ // MOBA_BLOCK)
    s_pad = nb * MOBA_BLOCK
    pad = [(0, 0), (0, 0), (0, s_pad - s_), (0, 0)]
    kp = jnp.pad(k, pad)
    vp = jnp.pad(v, pad)
    n_sel = min(MOBA_TOPK, nb)
    k_mean = jnp.mean(kp.reshape(b_, h_, nb, MOBA_BLOCK, dh).astype(jnp.float32), axis=3)
    gate = jnp.einsum("bhsd,bhnd->bhsn", q.astype(jnp.float32), k_mean)
    own = jnp.arange(s_) // MOBA_BLOCK
    past = jnp.arange(nb)[None, :] < own[:, None]
    gate = jnp.where(past[None, None], gate, -jnp.inf)
    _, sel = lax.top_k(gate, n_sel)
    scale = dh ** -0.5

    def attend_seq(args):
        q_s, kp_s, vp_s, sel_s = args
        kb = kp_s.reshape(h_, nb, MOBA_BLOCK, dh)
        vb = vp_s.reshape(h_, nb, MOBA_BLOCK, dh)

        def attend_qblock(qi):
            start = qi * Q_BLOCK
            qc = lax.dynamic_slice_in_dim(q_s, start, Q_BLOCK, axis=1)
            blk = start // MOBA_BLOCK
            k_own = lax.dynamic_index_in_dim(kb, blk, axis=1, keepdims=False)
            v_own = lax.dynamic_index_in_dim(vb, blk, axis=1, keepdims=False)
            idx = lax.dynamic_slice_in_dim(sel_s, start, Q_BLOCK, axis=1)
            k_sel = jax.vmap(lambda kb_h, i_h: kb_h[i_h])(kb, idx)
            v_sel = jax.vmap(lambda vb_h, i_h: vb_h[i_h])(vb, idx)
            qpos = start + jnp.arange(Q_BLOCK)
            kpos = blk * MOBA_BLOCK + jnp.arange(MOBA_BLOCK)
            s_own = jnp.einsum("hqd,hkd->hqk", qc, k_own).astype(jnp.float32) * scale
            s_own = jnp.where((kpos[None, :] <= qpos[:, None])[None], s_own, -jnp.inf)
            s_sel = jnp.einsum("hqd,hqnkd->hqnk", qc, k_sel).astype(jnp.float32) * scale
            ok = jnp.arange(n_sel) < blk
            s_sel = jnp.where(ok[None, None, :, None], s_sel, -jnp.inf)
            scores = jnp.concatenate([s_own, s_sel.reshape(h_, Q_BLOCK, n_sel * MOBA_BLOCK)], axis=-1)
            p = jax.nn.softmax(scores, axis=-1).astype(vp_s.dtype)
            p_own = p[..., :MOBA_BLOCK]
            p_sel = p[..., MOBA_BLOCK:].reshape(h_, Q_BLOCK, n_sel, MOBA_BLOCK)
            return (jnp.einsum("hqk,hkd->hqd", p_own, v_own)
                    + jnp.einsum("hqnk,hqnkd->hqd", p_sel, v_sel))

        o = lax.map(attend_qblock, jnp.arange(s_ // Q_BLOCK))
        return jnp.transpose(o, (1, 0, 2, 3)).reshape(h_, s_, dh)

    return lax.map(attend_seq, (q, kp, vp, sel))


def route(x_tok, w_router, router_bias):
    t_ = x_tok.shape[0]
    scores = jax.nn.sigmoid((x_tok @ w_router).astype(jnp.float32))
    biased = scores + router_bias.astype(jnp.float32)
    per_group = N_EXPERTS // N_GROUPS
    grp_score = jnp.sum(lax.top_k(biased.reshape(t_, N_GROUPS, per_group), 2)[0], axis=-1)
    _, grp_idx = lax.top_k(grp_score, TOPK_GROUPS)
    grp_mask = jnp.any(grp_idx[..., None] == jnp.arange(N_GROUPS), axis=-2)
    masked = jnp.where(jnp.repeat(grp_mask, per_group, axis=-1), biased, -jnp.inf)
    _, top_idx = lax.top_k(masked, TOP_K)
    w = jnp.take_along_axis(scores, top_idx, axis=-1)
    w = w / jnp.sum(w, -1, keepdims=True) * ROUTED_SCALE
    return top_idx, w


def routed_experts(x_tok, top_idx, top_w, w1, w3, w2):
    t_, d = x_tok.shape
    a = t_ * TOP_K
    eid = top_idx.reshape(a)
    tok = jnp.repeat(jnp.arange(t_, dtype=jnp.int32), TOP_K)
    wt = top_w.reshape(a)
    order = jnp.argsort(eid)
    e_s = eid[order]
    tok_s = tok[order]
    w_s = wt[order]
    counts = jnp.zeros((N_EXPERTS,), jnp.int32).at[eid].add(1)
    padded = (counts + MOE_BLOCK - 1) // MOE_BLOCK * MOE_BLOCK
    start = jnp.cumsum(counts) - counts
    pend = jnp.cumsum(padded)
    pstart = pend - padded
    rank = jnp.arange(a, dtype=jnp.int32) - start[e_s]
    dest = pstart[e_s] + rank
    step = MOE_BLOCK * MOE_CHUNK
    cap = -(-(a + N_EXPERTS * (MOE_BLOCK - 1)) // step) * step
    src = jnp.full((cap,), t_, jnp.int32).at[dest].set(tok_s)
    w_pad = jnp.zeros((cap,), x_tok.dtype).at[dest].set(w_s.astype(x_tok.dtype))
    x_ext = jnp.concatenate([x_tok, jnp.zeros((1, d), x_tok.dtype)], axis=0)
    x_pad = x_ext[src]
    n_blk = cap // MOE_BLOCK
    blk_exp = jnp.clip(jnp.searchsorted(pend, jnp.arange(n_blk, dtype=jnp.int32) * MOE_BLOCK, side="right"),
                       0, N_EXPERTS - 1)
    xb = x_pad.reshape(cap // step, MOE_CHUNK, MOE_BLOCK, d)
    eb = blk_exp.reshape(cap // step, MOE_CHUNK)

    def run_chunk(args):
        xc, ec = args
        hid = jax.nn.silu(jnp.einsum("gmd,gdf->gmf", xc, w1[ec])) * jnp.einsum("gmd,gdf->gmf", xc, w3[ec])
        return jnp.einsum("gmf,gfd->gmd", hid, w2[ec])

    y_pad = lax.map(run_chunk, (xb, eb)).reshape(cap, d)
    return jax.ops.segment_sum(y_pad * w_pad[:, None], src, num_segments=t_ + 1)[:t_]


def setup_inputs(seed: int = 0) -> dict:
    key = jax.random.key(seed)
    ks = jax.random.split(key, 20)
    L = DEPTH

    def nrm(k, shape, scale):
        return jax.random.normal(k, shape, jnp.float32) * scale

    return {
        "x": nrm(ks[0], (BATCH, SEQ, D_MODEL), 1.0),
        "w_in": nrm(ks[1], (L, D_MODEL, IN_COLS), D_MODEL ** -0.5),
        "b_gate": nrm(ks[2], (L, N_BRANCHES * D_MODEL), 0.02),
        "w_conv": nrm(ks[3], (L, CONV_K, CONV_WIDTH), CONV_K ** -0.5),
        "w_conv_out": nrm(ks[4], (L, CONV_WIDTH, D_MODEL), CONV_WIDTH ** -0.5),
        "w_attn_out": nrm(ks[5], (L, ATTN_WIDTH, D_MODEL), ATTN_WIDTH ** -0.5),
        "w_mix_out": nrm(ks[6], (L, D_MODEL, D_MODEL), D_MODEL ** -0.5 * DN_BETA),
        "ln1_g": 1.0 + nrm(ks[7], (L, D_MODEL), 0.02),
        "ln1_b": nrm(ks[8], (L, D_MODEL), 0.02),
        "w_router": nrm(ks[9], (L, D_MODEL, N_EXPERTS), D_MODEL ** -0.5),
        "router_bias": nrm(ks[10], (L, N_EXPERTS), 0.01),
        "w1": nrm(ks[11], (L, N_EXPERTS, D_MODEL, EXPERT_DIM), D_MODEL ** -0.5),
        "w3": nrm(ks[12], (L, N_EXPERTS, D_MODEL, EXPERT_DIM), D_MODEL ** -0.5),
        "w2": nrm(ks[13], (L, N_EXPERTS, EXPERT_DIM, D_MODEL), EXPERT_DIM ** -0.5 * DN_BETA),
        "ws1": nrm(ks[14], (L, D_MODEL, SHARED_DIM), D_MODEL ** -0.5),
        "ws3": nrm(ks[15], (L, D_MODEL, SHARED_DIM), D_MODEL ** -0.5),
        "ws2": nrm(ks[16], (L, SHARED_DIM, D_MODEL), SHARED_DIM ** -0.5 * DN_BETA),
        "ln2_g": 1.0 + nrm(ks[17], (L, D_MODEL), 0.02),
        "ln2_b": nrm(ks[18], (L, D_MODEL), 0.02),
    }


def reference(x, w_in, b_gate, w_conv, w_conv_out, w_attn_out, w_mix_out, ln1_g, ln1_b,
              w_router, router_bias, w1, w3, w2, ws1, ws3, ws2, ln2_g, ln2_b):
    b_, s_, d = x.shape
    pos = jnp.arange(s_)
    cuts = [CONV_WIDTH, 2 * CONV_WIDTH, 3 * CONV_WIDTH,
            3 * CONV_WIDTH + ATTN_WIDTH, 3 * CONV_WIDTH + 2 * ATTN_WIDTH, 3 * CONV_WIDTH + 3 * ATTN_WIDTH]
    h = x
    for l in range(DEPTH):
        proj = h @ w_in[l]
        cb, cc, cv, q, k, v, g = jnp.split(proj, cuts, axis=-1)
        gates = jax.nn.sigmoid(g + b_gate[l])
        g_conv, g_attn = gates[..., :d], gates[..., d:]
        y_conv = short_conv_mixer(cb, cc, cv, w_conv[l]) @ w_conv_out[l]

        def heads(t):
            return jnp.transpose(t.reshape(b_, s_, N_HEADS, HEAD_DIM), (0, 2, 1, 3))

        qh = partial_rope(heads(q), pos)
        kh = partial_rope(heads(k), pos)
        o = moba_attention(qh, kh, heads(v))
        y_attn = jnp.transpose(o, (0, 2, 1, 3)).reshape(b_, s_, ATTN_WIDTH) @ w_attn_out[l]
        mix = (g_conv * y_conv + g_attn * y_attn) @ w_mix_out[l]
        h = layer_norm(DN_ALPHA * h + mix, ln1_g[l], ln1_b[l])
        t = h.reshape(b_ * s_, d)
        top_idx, top_w = route(t, w_router[l], router_bias[l])
        routed = routed_experts(t, top_idx, top_w, w1[l], w3[l], w2[l])
        shared = (jax.nn.silu(t @ ws1[l]) * (t @ ws3[l])) @ ws2[l]
        h = layer_norm(DN_ALPHA * h + (routed + shared).reshape(b_, s_, d), ln2_g[l], ln2_b[l])
    return h
```

```python
import functools

import jax
import jax.numpy as jnp
from jax import lax
from jax.experimental import pallas as pl
from jax.experimental.pallas import tpu as pltpu

D_MODEL = 1024
CONV_WIDTH = 1024
CONV_K = 3
N_HEADS = 8
HEAD_DIM = 64
ATTN_WIDTH = N_HEADS * HEAD_DIM
MOBA_BLOCK = 256
MOBA_TOPK = 3
ROPE_THETA = 500000.0
ROPE_DIM = HEAD_DIM // 4
N_EXPERTS = 256
TOP_K = 8
N_GROUPS = 8
TOPK_GROUPS = 4
EXPERT_DIM = 256
SHARED_DIM = 256
ROUTED_SCALE = 2.5
DEPTH = 1
DN_ALPHA = (2 * DEPTH) ** 0.25
LN_EPS = 1e-5

NEG_BIG = -1e30

VMEM_LIMIT_BYTES = 56 * 1024 * 1024

PROJ_ROWS = 512
ROUTE_COLS = 256
EXPERT_ROWS = 256
DISPATCH_ROWS = 128
COMBINE_ROWS = 128

F32 = jnp.float32
BF16 = jnp.bfloat16
U32 = jnp.uint32
I32 = jnp.int32


def _sigmoid(v):
    return 1.0 / (1.0 + jnp.exp(-v))


def _dot(a, b):
    return jnp.dot(a, b, preferred_element_type=F32)


def _dot_nt(a, b):
    return lax.dot_general(a, b, (((1,), (1,)), ((), ())), preferred_element_type=F32)


def _dot_tn(a, b):
    return lax.dot_general(a, b, (((0,), (0,)), ((), ())), preferred_element_type=F32)


def _pack_bf16_pair(left, right):
    hi = pltpu.bitcast(left.astype(BF16).astype(F32), U32)
    lo = pltpu.bitcast(right.astype(BF16).astype(F32), U32)
    return hi | (lo >> 16)


def _unpack_bf16_pair(u):
    left = pltpu.bitcast(u & jnp.uint32(0xFFFF0000), F32)
    right = pltpu.bitcast(u << 16, F32)
    return left, right


def _layer_norm(r, g, b):
    mu = jnp.mean(r, axis=-1, keepdims=True)
    c = r - mu
    var = jnp.mean(c * c, axis=-1, keepdims=True)
    return c * lax.rsqrt(var + LN_EPS) * g + b


def _proj_kernel(x_ref, wc_ref, wqkv_ref, wg_ref, bg_ref, wconv_ref, wco_ref,
                 cos_ref, sinlo_ref, sinhi_ref,
                 a_ref, gattn_ref, q_ref, k_ref, v_ref, kmean_ref,
                 halo_ref, *, tiles_per_seq):
    tm = x_ref.shape[0]
    i = pl.program_id(0)
    xb = x_ref[...].astype(BF16)

    cb = _dot(xb, wc_ref[:, 0:CONV_WIDTH])
    cc = _dot(xb, wc_ref[:, CONV_WIDTH:2 * CONV_WIDTH])
    cv = _dot(xb, wc_ref[:, 2 * CONV_WIDTH:3 * CONV_WIDTH])
    u = cc * cv
    @pl.when((i % tiles_per_seq) == 0)
    def _():
        halo_ref[...] = jnp.zeros(halo_ref.shape, F32)

    prev = halo_ref[...]
    row = lax.broadcasted_iota(I32, u.shape, 0)
    u1 = jnp.where(row == 0, prev[7:8, :], pltpu.roll(u, 1, 0))
    u2 = jnp.where(row == 0, prev[6:7, :], jnp.where(row == 1, prev[7:8, :], pltpu.roll(u, 2, 0)))
    halo_ref[...] = u[tm - 8:tm, :]
    w0 = wconv_ref[0:1, :]
    w1 = wconv_ref[1:2, :]
    w2 = wconv_ref[2:3, :]
    yc = cb * (w0 * u2 + w1 * u1 + w2 * u)
    y_conv = _dot(yc.astype(BF16), wco_ref[...])

    g = _dot(xb, wg_ref[...]) + bg_ref[...]
    gates = _sigmoid(g)
    a_ref[...] = (gates[:, :D_MODEL] * y_conv).astype(a_ref.dtype)
    gattn_ref[...] = gates[:, D_MODEL:].astype(gattn_ref.dtype)

    qkv = _dot(xb, wqkv_ref[...])
    cos = cos_ref[...]
    sinlo = sinlo_ref[...]
    sinhi = sinhi_ref[...]

    def rope(t):
        return (t * cos + pltpu.roll(t, ATTN_WIDTH - ROPE_DIM // 2, 1) * sinlo
                + pltpu.roll(t, ROPE_DIM // 2, 1) * sinhi)

    q = rope(qkv[:, 0:ATTN_WIDTH])
    k = rope(qkv[:, ATTN_WIDTH:2 * ATTN_WIDTH])
    q_ref[...] = q.astype(q_ref.dtype)
    k_ref[...] = k.astype(k_ref.dtype)
    v_ref[...] = qkv[:, 2 * ATTN_WIDTH:].astype(v_ref.dtype)
    for blk in range(tm // MOBA_BLOCK):
        kb = k[blk * MOBA_BLOCK:(blk + 1) * MOBA_BLOCK, :]
        kmean_ref[blk] = jnp.mean(kb, axis=0, keepdims=True)


def _proj(x2, wc, wqkv, wg, bg, wconv, wco, cos, sinlo, sinhi, seq):
    t = x2.shape[0]
    tm = PROJ_ROWS
    tiles_per_seq = seq // tm
    const = lambda i: (0, 0)
    rows = lambda i: (i, 0)
    pos = lambda i: (i % tiles_per_seq, 0)
    return pl.pallas_call(
        functools.partial(_proj_kernel, tiles_per_seq=tiles_per_seq),
        name="proj",
        grid=(t // tm,),
        in_specs=[
            pl.BlockSpec((tm, D_MODEL), rows),
            pl.BlockSpec(wc.shape, const),
            pl.BlockSpec(wqkv.shape, const),
            pl.BlockSpec(wg.shape, const),
            pl.BlockSpec(bg.shape, const),
            pl.BlockSpec(wconv.shape, const),
            pl.BlockSpec(wco.shape, const),
            pl.BlockSpec((tm, ATTN_WIDTH), pos),
            pl.BlockSpec((tm, ATTN_WIDTH), pos),
            pl.BlockSpec((tm, ATTN_WIDTH), pos),
        ],
        out_specs=[
            pl.BlockSpec((tm, D_MODEL), rows),
            pl.BlockSpec((tm, D_MODEL), rows),
            pl.BlockSpec((tm, ATTN_WIDTH), rows),
            pl.BlockSpec((tm, ATTN_WIDTH), rows),
            pl.BlockSpec((tm, ATTN_WIDTH), rows),
            pl.BlockSpec((tm // MOBA_BLOCK, 1, ATTN_WIDTH), lambda i: (i, 0, 0)),
        ],
        out_shape=[
            jax.ShapeDtypeStruct((t, D_MODEL), BF16),
            jax.ShapeDtypeStruct((t, D_MODEL), BF16),
            jax.ShapeDtypeStruct((t, ATTN_WIDTH), BF16),
            jax.ShapeDtypeStruct((t, ATTN_WIDTH), BF16),
            jax.ShapeDtypeStruct((t, ATTN_WIDTH), BF16),
            jax.ShapeDtypeStruct((t // MOBA_BLOCK, 1, ATTN_WIDTH), F32),
        ],
        scratch_shapes=[pltpu.VMEM((8, CONV_WIDTH), F32)],
        compiler_params=pltpu.CompilerParams(
            dimension_semantics=("arbitrary",), vmem_limit_bytes=VMEM_LIMIT_BYTES),
    )(x2, wc, wqkv, wg, bg, wconv, wco, cos, sinlo, sinhi)


def _attn_kernel(q_ref, k_ref, v_ref, kmean_ref, o_ref, bias_ref, ot_ref, *, n_blocks):
    j = pl.program_id(1)
    blk_rows = MOBA_BLOCK
    scale = HEAD_DIM ** -0.5
    key_i = lax.broadcasted_iota(I32, (blk_rows, blk_rows), 0)
    qry_i = lax.broadcasted_iota(I32, (blk_rows, blk_rows), 1)
    causal = key_i <= qry_i
    blk_i = lax.broadcasted_iota(I32, (n_blocks, blk_rows), 0)
    past = blk_i < j

    for h in range(N_HEADS):
        cols = slice(h * HEAD_DIM, (h + 1) * HEAD_DIM)
        qh = q_ref[:, cols]
        km = kmean_ref[:, 0, cols]
        gate = _dot_nt(km, qh.astype(F32))
        gate = jnp.where(past, gate, -jnp.inf)
        rank = jnp.zeros(gate.shape, F32)
        for m in range(n_blocks):
            gm = gate[m:m + 1, :]
            ahead = jnp.where(gm > gate, 1.0, jnp.where(gm == gate, jnp.where(blk_i > m, 1.0, 0.0), 0.0))
            rank = rank + ahead
        sel = jnp.where(past, jnp.where(rank < MOBA_TOPK, 0.0, NEG_BIG), NEG_BIG)
        bias_ref[...] = sel

        row0 = pl.multiple_of(j * blk_rows, blk_rows)
        kj = k_ref[pl.ds(row0, blk_rows), cols]
        vj = v_ref[pl.ds(row0, blk_rows), cols]
        s = _dot_nt(kj, qh) * scale
        s = jnp.where(causal, s, NEG_BIG)
        m0 = jnp.max(s, axis=0, keepdims=True)
        p = jnp.exp(s - m0)
        l0 = jnp.sum(p, axis=0, keepdims=True)
        acc0 = _dot_tn(vj, p.astype(vj.dtype))

        def body(n, carry):
            m_run, l_run, acc = carry
            r = pl.multiple_of(n * blk_rows, blk_rows)
            kn = k_ref[pl.ds(r, blk_rows), cols]
            vn = v_ref[pl.ds(r, blk_rows), cols]
            sn = _dot_nt(kn, qh) * scale + bias_ref[pl.ds(n, 1), :]
            m_new = jnp.maximum(m_run, jnp.max(sn, axis=0, keepdims=True))
            alpha = jnp.exp(m_run - m_new)
            pn = jnp.exp(sn - m_new)
            l_new = alpha * l_run + jnp.sum(pn, axis=0, keepdims=True)
            acc_new = acc * alpha + _dot_tn(vn, pn.astype(vn.dtype))
            return m_new, l_new, acc_new

        _, l_fin, acc_fin = lax.fori_loop(0, j, body, (m0, l0, acc0))
        ot_ref[cols, :] = acc_fin / l_fin

    o_ref[...] = ot_ref[...].T.astype(o_ref.dtype)


def _attention(q, k, v, kmean, batch, seq):
    n_blocks = seq // MOBA_BLOCK
    return pl.pallas_call(
        functools.partial(_attn_kernel, n_blocks=n_blocks),
        name="attention",
        grid=(batch, n_blocks),
        in_specs=[
            pl.BlockSpec((MOBA_BLOCK, ATTN_WIDTH), lambda b, j: (b * n_blocks + j, 0)),
            pl.BlockSpec((seq, ATTN_WIDTH), lambda b, j: (b, 0)),
            pl.BlockSpec((seq, ATTN_WIDTH), lambda b, j: (b, 0)),
            pl.BlockSpec((n_blocks, 1, ATTN_WIDTH), lambda b, j: (b, 0, 0)),
        ],
        out_specs=pl.BlockSpec((MOBA_BLOCK, ATTN_WIDTH), lambda b, j: (b * n_blocks + j, 0)),
        out_shape=jax.ShapeDtypeStruct(q.shape, BF16),
        scratch_shapes=[
            pltpu.VMEM((n_blocks, MOBA_BLOCK), F32),
            pltpu.VMEM((ATTN_WIDTH, MOBA_BLOCK), F32),
        ],
        compiler_params=pltpu.CompilerParams(
            dimension_semantics=("arbitrary", "arbitrary"), vmem_limit_bytes=VMEM_LIMIT_BYTES),
    )(q, k, v, kmean)


def _post_kernel(o_ref, a_ref, gattn_ref, x_ref, wao_ref, wmo_ref, g1_ref, b1_ref,
                 ws1_ref, ws3_ref, ws2_ref, wrt_ref,
                 base_ref, hp_ref, logit_ref):
    y_attn = _dot(o_ref[...], wao_ref[...])
    merged = a_ref[...].astype(F32) + gattn_ref[...].astype(F32) * y_attn
    mix = _dot(merged.astype(BF16), wmo_ref[...])
    h = _layer_norm(DN_ALPHA * x_ref[...] + mix, g1_ref[...], b1_ref[...])
    hb = h.astype(BF16)
    s1 = _dot(hb, ws1_ref[...])
    s3 = _dot(hb, ws3_ref[...])
    shared = _dot((s1 * _sigmoid(s1) * s3).astype(BF16), ws2_ref[...])
    base_ref[...] = DN_ALPHA * h + shared
    half = D_MODEL // 2
    hp_ref[...] = _pack_bf16_pair(h[:, :half], h[:, half:])
    logit_ref[...] = _dot_nt(wrt_ref[...], h)


def _post(o, a, gattn, x2, wao, wmo, g1, b1, ws1, ws3, ws2, wrt):
    t = x2.shape[0]
    tm = PROJ_ROWS
    const = lambda i: (0, 0)
    rows = lambda i: (i, 0)
    return pl.pallas_call(
        _post_kernel,
        name="post",
        grid=(t // tm,),
        in_specs=[
            pl.BlockSpec((tm, ATTN_WIDTH), rows),
            pl.BlockSpec((tm, D_MODEL), rows),
            pl.BlockSpec((tm, D_MODEL), rows),
            pl.BlockSpec((tm, D_MODEL), rows),
            pl.BlockSpec(wao.shape, const),
            pl.BlockSpec(wmo.shape, const),
            pl.BlockSpec(g1.shape, const),
            pl.BlockSpec(b1.shape, const),
            pl.BlockSpec(ws1.shape, const),
            pl.BlockSpec(ws3.shape, const),
            pl.BlockSpec(ws2.shape, const),
            pl.BlockSpec(wrt.shape, const),
        ],
        out_specs=[
            pl.BlockSpec((tm, D_MODEL), rows),
            pl.BlockSpec((tm, D_MODEL // 2), rows),
            pl.BlockSpec((N_EXPERTS, tm), lambda i: (0, i)),
        ],
        out_shape=[
            jax.ShapeDtypeStruct((t, D_MODEL), F32),
            jax.ShapeDtypeStruct((t, D_MODEL // 2), U32),
            jax.ShapeDtypeStruct((N_EXPERTS, t), F32),
        ],
        compiler_params=pltpu.CompilerParams(
            dimension_semantics=("arbitrary",), vmem_limit_bytes=VMEM_LIMIT_BYTES),
    )(o, a, gattn, x2, wao, wmo, g1, b1, ws1, ws3, ws2, wrt)


def _route_kernel(logit_ref, bias_ref, tri_ref, eid_ref, pos_ref, wt_ref, cnt_ref, run_ref):
    i = pl.program_id(0)
    tm = logit_ref.shape[1]
    per_group = N_EXPERTS // N_GROUPS

    @pl.when(i == 0)
    def _():
        run_ref[...] = jnp.zeros(run_ref.shape, F32)

    scores = _sigmoid(logit_ref[...])
    biased = scores + bias_ref[...]
    b3 = biased.reshape(N_GROUPS, per_group, tm)
    i3 = lax.broadcasted_iota(I32, b3.shape, 1)
    m1 = jnp.max(b3, axis=1)
    first = jnp.min(jnp.where(b3 == m1[:, None, :], i3, per_group), axis=1)
    m2 = jnp.max(jnp.where(i3 == first[:, None, :], -jnp.inf, b3), axis=1)
    gscore = m1 + m2
    g_i = lax.broadcasted_iota(I32, gscore.shape, 0)
    grank = jnp.zeros(gscore.shape, F32)
    for m in range(N_GROUPS):
        gm = gscore[m:m + 1, :]
        ahead = jnp.where(gm > gscore, 1.0, jnp.where(gm == gscore, jnp.where(g_i > m, 1.0, 0.0), 0.0))
        grank = grank + ahead
    gsel = grank < TOPK_GROUPS
    work = jnp.where(gsel[:, None, :], b3, -jnp.inf).reshape(N_EXPERTS, tm)

    e_i = lax.broadcasted_iota(I32, (N_EXPERTS, tm), 0)
    sel = jnp.zeros((N_EXPERTS, tm), F32)
    idxs = []
    scs = []
    for _k in range(TOP_K):
        mx = jnp.max(work, axis=0, keepdims=True)
        idx = jnp.min(jnp.where(work == mx, e_i, N_EXPERTS), axis=0, keepdims=True)
        hit = e_i == idx
        scs.append(jnp.sum(jnp.where(hit, scores, 0.0), axis=0, keepdims=True))
        sel = sel + jnp.where(hit, 1.0, 0.0)
        work = jnp.where(hit, -jnp.inf, work)
        idxs.append(idx)

    rank = _dot(sel.astype(BF16), tri_ref[...]) + run_ref[...]
    run_ref[...] = run_ref[...] + jnp.sum(sel, axis=1, keepdims=True)
    cnt_ref[...] = run_ref[...]

    wsum = scs[0]
    for kk in range(1, TOP_K):
        wsum = wsum + scs[kk]
    for kk in range(TOP_K):
        pos = jnp.sum(jnp.where(e_i == idxs[kk], rank, 0.0), axis=0, keepdims=True)
        eid_ref[kk:kk + 1, :] = idxs[kk]
        pos_ref[kk:kk + 1, :] = pos.astype(I32)
        wt_ref[kk:kk + 1, :] = scs[kk] / wsum * ROUTED_SCALE


def _route(logits_t, bias_col, tri):
    t = logits_t.shape[1]
    tm = ROUTE_COLS
    cols = lambda i: (0, i)
    const = lambda i: (0, 0)
    return pl.pallas_call(
        _route_kernel,
        name="route",
        grid=(t // tm,),
        in_specs=[
            pl.BlockSpec((N_EXPERTS, tm), cols),
            pl.BlockSpec((N_EXPERTS, 1), const),
            pl.BlockSpec((tm, tm), const),
        ],
        out_specs=[
            pl.BlockSpec((TOP_K, tm), cols),
            pl.BlockSpec((TOP_K, tm), cols),
            pl.BlockSpec((TOP_K, tm), cols),
            pl.BlockSpec((N_EXPERTS, 1), const),
        ],
        out_shape=[
            jax.ShapeDtypeStruct((TOP_K, t), I32),
            jax.ShapeDtypeStruct((TOP_K, t), I32),
            jax.ShapeDtypeStruct((TOP_K, t), F32),
            jax.ShapeDtypeStruct((N_EXPERTS, 1), F32),
        ],
        scratch_shapes=[pltpu.VMEM((N_EXPERTS, 1), F32)],
        compiler_params=pltpu.CompilerParams(
            dimension_semantics=("arbitrary",), vmem_limit_bytes=VMEM_LIMIT_BYTES),
    )(logits_t, bias_col, tri)


def _dispatch_kernel(dest_ref, hp_ref, xpad_in_ref, xpad_ref, sem):
    del xpad_in_ref
    i = pl.program_id(0)
    n = pl.num_programs(0)
    tm = DISPATCH_ROWS
    base = i * tm

    def row_copy(src_row, dst_row):
        return pltpu.make_async_copy(hp_ref.at[pl.ds(src_row, 1)], xpad_ref.at[pl.ds(dst_row, 1)], sem)

    def issue(r, carry):
        for kk in range(TOP_K):
            row_copy(base + r, dest_ref[r * TOP_K + kk]).start()
        return carry

    lax.fori_loop(0, tm, issue, 0)

    def drain(_r, carry):
        for _kk in range(TOP_K):
            row_copy(0, 0).wait()
        return carry

    @pl.when(i > 0)
    def _():
        lax.fori_loop(0, tm, drain, 0)

    @pl.when(i == n - 1)
    def _():
        lax.fori_loop(0, tm, drain, 0)


def _dispatch(dest_flat, hp, cap):
    t = hp.shape[0]
    tm = DISPATCH_ROWS
    xpad0 = jnp.zeros((cap, hp.shape[1]), hp.dtype)
    return pl.pallas_call(
        _dispatch_kernel,
        name="dispatch",
        grid=(t // tm,),
        in_specs=[
            pl.BlockSpec((tm * TOP_K,), lambda i: (i,), memory_space=pltpu.SMEM),
            pl.BlockSpec(memory_space=pl.ANY),
            pl.BlockSpec(memory_space=pl.ANY),
        ],
        out_specs=pl.BlockSpec(memory_space=pl.ANY),
        out_shape=jax.ShapeDtypeStruct(xpad0.shape, xpad0.dtype),
        scratch_shapes=[pltpu.SemaphoreType.DMA(())],
        input_output_aliases={2: 0},
        compiler_params=pltpu.CompilerParams(
            dimension_semantics=("arbitrary",), has_side_effects=True),
    )(dest_flat, hp, xpad0)


def _expert_kernel(texp_ref, nvalid_ref, x_ref, w1_ref, w3_ref, w2_ref, y_ref, w1b_ref, w3b_ref, w2b_ref):
    i = pl.program_id(0)
    valid = i < nvalid_ref[0]
    prev = texp_ref[jnp.maximum(i - 1, 0)]
    fresh = jnp.logical_or(i == 0, texp_ref[i] != prev)

    @pl.when(jnp.logical_and(valid, fresh))
    def _():
        w1b_ref[...] = w1_ref[...].astype(BF16)
        w3b_ref[...] = w3_ref[...].astype(BF16)
        w2b_ref[...] = w2_ref[...].astype(BF16)

    @pl.when(valid)
    def _():
        left, right = _unpack_bf16_pair(x_ref[...])
        xb = jnp.concatenate([left.astype(BF16), right.astype(BF16)], axis=1)
        g = _dot(xb, w1b_ref[...])
        u = _dot(xb, w3b_ref[...])
        hid = (g * _sigmoid(g) * u).astype(BF16)
        y = _dot(hid, w2b_ref[...])
        half = D_MODEL // 2
        y_ref[...] = _pack_bf16_pair(y[:, :half], y[:, half:])

    @pl.when(jnp.logical_not(valid))
    def _():
        y_ref[...] = jnp.zeros(y_ref.shape, y_ref.dtype)


def _experts(tile_expert, n_valid, xpad, w1, w3, w2):
    cap = xpad.shape[0]
    tm = EXPERT_ROWS
    n_tiles = cap // tm

    def x_map(i, te, nv):
        return (jnp.minimum(i, jnp.maximum(nv[0] - 1, 0)), 0)

    def w_map(i, te, nv):
        return (te[i], 0, 0)

    grid_spec = pltpu.PrefetchScalarGridSpec(
        num_scalar_prefetch=2,
        grid=(n_tiles,),
        in_specs=[
            pl.BlockSpec((tm, D_MODEL // 2), x_map),
            pl.BlockSpec((None, D_MODEL, EXPERT_DIM), w_map),
            pl.BlockSpec((None, D_MODEL, EXPERT_DIM), w_map),
            pl.BlockSpec((None, EXPERT_DIM, D_MODEL), w_map),
        ],
        out_specs=pl.BlockSpec((tm, D_MODEL // 2), lambda i, te, nv: (i, 0)),
        scratch_shapes=[
            pltpu.VMEM((D_MODEL, EXPERT_DIM), BF16),
            pltpu.VMEM((D_MODEL, EXPERT_DIM), BF16),
            pltpu.VMEM((EXPERT_DIM, D_MODEL), BF16),
        ],
    )
    return pl.pallas_call(
        _expert_kernel,
        name="experts",
        grid_spec=grid_spec,
        out_shape=jax.ShapeDtypeStruct(xpad.shape, U32),
        compiler_params=pltpu.CompilerParams(
            dimension_semantics=("arbitrary",), vmem_limit_bytes=VMEM_LIMIT_BYTES),
    )(tile_expert, n_valid, xpad, w1, w3, w2)


def _combine_kernel(dest_ref, dest_next_ref, wt_ref, base_ref, g2_ref, b2_ref, ypad_ref, out_ref, buf_ref, sem):
    i = pl.program_id(0)
    n = pl.num_programs(0)
    tm = COMBINE_ROWS
    slot = i % 2

    def row_copy(src_row, s, kk, r):
        return pltpu.make_async_copy(ypad_ref.at[pl.ds(src_row, 1)], buf_ref.at[s, kk, pl.ds(r, 1)], sem.at[s])

    def issue_from(d_ref, s):
        def issue(r, carry):
            for kk in range(TOP_K):
                row_copy(d_ref[r * TOP_K + kk], s, kk, r).start()
            return carry
        lax.fori_loop(0, tm, issue, 0)

    @pl.when(i == 0)
    def _():
        issue_from(dest_ref, 0)

    @pl.when(i + 1 < n)
    def _():
        issue_from(dest_next_ref, 1 - slot)

    def drain(r, carry):
        for kk in range(TOP_K):
            row_copy(0, slot, kk, r).wait()
        return carry

    lax.fori_loop(0, tm, drain, 0)

    wt = wt_ref[...]
    half = D_MODEL // 2
    base = base_ref[...]
    acc_l = base[:, :half]
    acc_r = base[:, half:]
    for kk in range(TOP_K):
        left, right = _unpack_bf16_pair(buf_ref[slot, kk])
        wk = wt[:, kk:kk + 1]
        acc_l = acc_l + wk * left
        acc_r = acc_r + wk * right
    r = jnp.concatenate([acc_l, acc_r], axis=1)
    out_ref[...] = _layer_norm(r, g2_ref[...], b2_ref[...])


def _combine(dest_flat, wt_tok, base, g2, b2, ypad):
    t = base.shape[0]
    tm = COMBINE_ROWS
    n = t // tm
    rows = lambda i: (i, 0)
    const = lambda i: (0, 0)
    return pl.pallas_call(
        _combine_kernel,
        name="combine",
        grid=(n,),
        in_specs=[
            pl.BlockSpec((tm * TOP_K,), lambda i: (i,), memory_space=pltpu.SMEM),
            pl.BlockSpec((tm * TOP_K,), lambda i: (jnp.minimum(i + 1, n - 1),), memory_space=pltpu.SMEM),
            pl.BlockSpec((tm, TOP_K), rows),
            pl.BlockSpec((tm, D_MODEL), rows),
            pl.BlockSpec(g2.shape, const),
            pl.BlockSpec(b2.shape, const),
            pl.BlockSpec(memory_space=pl.ANY),
        ],
        out_specs=pl.BlockSpec((tm, D_MODEL), rows),
        out_shape=jax.ShapeDtypeStruct((t, D_MODEL), F32),
        scratch_shapes=[
            pltpu.VMEM((2, TOP_K, tm, D_MODEL // 2), U32),
            pltpu.SemaphoreType.DMA((2,)),
        ],
        compiler_params=pltpu.CompilerParams(
            dimension_semantics=("arbitrary",), vmem_limit_bytes=VMEM_LIMIT_BYTES),
    )(dest_flat, dest_flat, wt_tok, base, g2, b2, ypad)


def _rope_tables(seq):
    half = ROPE_DIM // 2
    inv = jnp.power(ROPE_THETA, -jnp.arange(half, dtype=F32) * 2.0 / ROPE_DIM)
    ang = jnp.arange(seq, dtype=F32)[:, None] * inv[None, :]
    cos = jnp.cos(ang)
    sin = jnp.sin(ang)
    ones = jnp.ones((seq, HEAD_DIM - ROPE_DIM), F32)
    zeros_rest = jnp.zeros((seq, HEAD_DIM - ROPE_DIM), F32)
    zeros_half = jnp.zeros((seq, half), F32)
    cos_h = jnp.concatenate([cos, cos, ones], axis=1)
    sinlo_h = jnp.concatenate([-sin, zeros_half, zeros_rest], axis=1)
    sinhi_h = jnp.concatenate([zeros_half, sin, zeros_rest], axis=1)
    tile = lambda m: jnp.tile(m, (1, N_HEADS))
    return tile(cos_h), tile(sinlo_h), tile(sinhi_h)


def kernel(x, w_in, b_gate, w_conv, w_conv_out, w_attn_out, w_mix_out, ln1_g, ln1_b, w_router, router_bias,
           w1, w3, w2, ws1, ws3, ws2, ln2_g, ln2_b):
    batch, seq, d = x.shape
    t = batch * seq
    assert d == D_MODEL and seq % PROJ_ROWS == 0 and seq % MOBA_BLOCK == 0
    x2 = x.reshape(t, d)
    h = None
    for layer in range(DEPTH):
        c3 = 3 * CONV_WIDTH
        a3 = 3 * ATTN_WIDTH
        w_in_b = w_in[layer].astype(BF16)
        wc = w_in_b[:, :c3]
        wqkv = w_in_b[:, c3:c3 + a3]
        wg = w_in_b[:, c3 + a3:]
        cos, sinlo, sinhi = _rope_tables(seq)
        a, gattn, q, k, v, kmean = _proj(
            x2, wc, wqkv, wg, b_gate[layer][None, :], w_conv[layer], w_conv_out[layer].astype(BF16),
            cos, sinlo, sinhi, seq)
        o = _attention(q, k, v, kmean, batch, seq)
        base, hp, logits_t = _post(
            o, a, gattn, x2, w_attn_out[layer].astype(BF16), w_mix_out[layer].astype(BF16),
            ln1_g[layer][None, :], ln1_b[layer][None, :],
            ws1[layer].astype(BF16), ws3[layer].astype(BF16), ws2[layer].astype(BF16),
            w_router[layer].T)

        tri = (jnp.arange(ROUTE_COLS)[:, None] < jnp.arange(ROUTE_COLS)[None, :]).astype(BF16)
        eid, pos, wts, counts = _route(logits_t, router_bias[layer][:, None], tri)

        cnt = counts[:, 0].astype(I32)
        padded = (cnt + EXPERT_ROWS - 1) // EXPERT_ROWS * EXPERT_ROWS
        pend = jnp.cumsum(padded)
        pstart = pend - padded
        cap = t * TOP_K + N_EXPERTS * EXPERT_ROWS
        n_tiles = cap // EXPERT_ROWS
        dest = pstart[eid] + pos
        dest_flat = dest.T.reshape(t * TOP_K)
        tile_expert = jnp.clip(
            jnp.searchsorted(pend, jnp.arange(n_tiles, dtype=I32) * EXPERT_ROWS, side="right"),
            0, N_EXPERTS - 1).astype(I32)
        n_valid = (pend[-1:] // EXPERT_ROWS).astype(I32)

        xpad = _dispatch(dest_flat, hp, cap)
        ypad = _experts(tile_expert, n_valid, xpad, w1[layer], w3[layer], w2[layer])
        h = _combine(dest_flat, wts.T, base, ln2_g[layer][None, :], ln2_b[layer][None, :], ypad)
        x2 = h
    return h.reshape(batch, seq, d)
```

```python
import functools

import jax
import jax.numpy as jnp
from jax import lax
from jax.experimental import pallas as pl
from jax.experimental.pallas import tpu as pltpu

D_MODEL = 1024
CONV_WIDTH = 1024
CONV_K = 3
N_HEADS = 8
HEAD_DIM = 64
ATTN_WIDTH = N_HEADS * HEAD_DIM
MOBA_BLOCK = 256
MOBA_TOPK = 3
ROPE_THETA = 500000.0
ROPE_DIM = HEAD_DIM // 4
N_EXPERTS = 256
TOP_K = 8
N_GROUPS = 8
TOPK_GROUPS = 4
EXPERT_DIM = 256
SHARED_DIM = 256
ROUTED_SCALE = 2.5
DEPTH = 1
DN_ALPHA = (2 * DEPTH) ** 0.25
LN_EPS = 1e-5

NEG_BIG = -1e30

VMEM_LIMIT_BYTES = 56 * 1024 * 1024

PROJ_ROWS = 512
ROUTE_COLS = 256
EXPERT_ROWS = 256
DISPATCH_ROWS = 128
COMBINE_ROWS = 128

F32 = jnp.float32
BF16 = jnp.bfloat16
U32 = jnp.uint32
I32 = jnp.int32


def _sigmoid(v):
    return 1.0 / (1.0 + jnp.exp(-v))


def _dot(a, b):
    return jnp.dot(a, b, preferred_element_type=F32)


def _dot_nt(a, b):
    return lax.dot_general(a, b, (((1,), (1,)), ((), ())), preferred_element_type=F32)


def _dot_tn(a, b):
    return lax.dot_general(a, b, (((0,), (0,)), ((), ())), preferred_element_type=F32)


ROW_TILE = 8
LANES = 128


def _store_row_tiles(ref, val):
    m = val.shape[0]
    for s in range(ROW_TILE):
        ref[pl.ds(s, m, stride=ROW_TILE), :] = val[:, s * LANES:(s + 1) * LANES]


def _load_row_tiles(ref, m, dtype=None):
    parts = [ref[pl.ds(s, m, stride=ROW_TILE), :] for s in range(ROW_TILE)]
    if dtype is not None:
        parts = [p.astype(dtype) for p in parts]
    return jnp.concatenate(parts, axis=1)


def _layer_norm(r, g, b):
    mu = jnp.mean(r, axis=-1, keepdims=True)
    c = r - mu
    var = jnp.mean(c * c, axis=-1, keepdims=True)
    return c * lax.rsqrt(var + LN_EPS) * g + b


def _proj_kernel(x_ref, wc_ref, wqkv_ref, wg_ref, bg_ref, wconv_ref, wco_ref,
                 cos_ref, sinlo_ref, sinhi_ref,
                 a_ref, gattn_ref, q_ref, k_ref, v_ref, kmean_ref,
                 halo_ref, *, tiles_per_seq):
    tm = x_ref.shape[0]
    i = pl.program_id(0)
    xb = x_ref[...].astype(BF16)

    cb = _dot(xb, wc_ref[:, 0:CONV_WIDTH])
    cc = _dot(xb, wc_ref[:, CONV_WIDTH:2 * CONV_WIDTH])
    cv = _dot(xb, wc_ref[:, 2 * CONV_WIDTH:3 * CONV_WIDTH])
    u = cc * cv
    @pl.when((i % tiles_per_seq) == 0)
    def _():
        halo_ref[...] = jnp.zeros(halo_ref.shape, F32)

    prev = halo_ref[...]
    row = lax.broadcasted_iota(I32, u.shape, 0)
    u1 = jnp.where(row == 0, prev[7:8, :], pltpu.roll(u, 1, 0))
    u2 = jnp.where(row == 0, prev[6:7, :], jnp.where(row == 1, prev[7:8, :], pltpu.roll(u, 2, 0)))
    halo_ref[...] = u[tm - 8:tm, :]
    w0 = wconv_ref[0:1, :]
    w1 = wconv_ref[1:2, :]
    w2 = wconv_ref[2:3, :]
    yc = cb * (w0 * u2 + w1 * u1 + w2 * u)
    y_conv = _dot(yc.astype(BF16), wco_ref[...])

    g = _dot(xb, wg_ref[...]) + bg_ref[...]
    gates = _sigmoid(g)
    a_ref[...] = (gates[:, :D_MODEL] * y_conv).astype(a_ref.dtype)
    gattn_ref[...] = gates[:, D_MODEL:].astype(gattn_ref.dtype)

    qkv = _dot(xb, wqkv_ref[...])
    cos = cos_ref[...]
    sinlo = sinlo_ref[...]
    sinhi = sinhi_ref[...]

    def rope(t):
        return (t * cos + pltpu.roll(t, ATTN_WIDTH - ROPE_DIM // 2, 1) * sinlo
                + pltpu.roll(t, ROPE_DIM // 2, 1) * sinhi)

    q = rope(qkv[:, 0:ATTN_WIDTH])
    k = rope(qkv[:, ATTN_WIDTH:2 * ATTN_WIDTH])
    q_ref[...] = q.astype(q_ref.dtype)
    k_ref[...] = k.astype(k_ref.dtype)
    v_ref[...] = qkv[:, 2 * ATTN_WIDTH:].astype(v_ref.dtype)
    for blk in range(tm // MOBA_BLOCK):
        kb = k[blk * MOBA_BLOCK:(blk + 1) * MOBA_BLOCK, :]
        kmean_ref[blk] = jnp.mean(kb, axis=0, keepdims=True)


def _proj(x2, wc, wqkv, wg, bg, wconv, wco, cos, sinlo, sinhi, seq):
    t = x2.shape[0]
    tm = PROJ_ROWS
    tiles_per_seq = seq // tm
    const = lambda i: (0, 0)
    rows = lambda i: (i, 0)
    pos = lambda i: (i % tiles_per_seq, 0)
    return pl.pallas_call(
        functools.partial(_proj_kernel, tiles_per_seq=tiles_per_seq),
        name="proj",
        grid=(t // tm,),
        in_specs=[
            pl.BlockSpec((tm, D_MODEL), rows),
            pl.BlockSpec(wc.shape, const),
            pl.BlockSpec(wqkv.shape, const),
            pl.BlockSpec(wg.shape, const),
            pl.BlockSpec(bg.shape, const),
            pl.BlockSpec(wconv.shape, const),
            pl.BlockSpec(wco.shape, const),
            pl.BlockSpec((tm, ATTN_WIDTH), pos),
            pl.BlockSpec((tm, ATTN_WIDTH), pos),
            pl.BlockSpec((tm, ATTN_WIDTH), pos),
        ],
        out_specs=[
            pl.BlockSpec((tm, D_MODEL), rows),
            pl.BlockSpec((tm, D_MODEL), rows),
            pl.BlockSpec((tm, ATTN_WIDTH), rows),
            pl.BlockSpec((tm, ATTN_WIDTH), rows),
            pl.BlockSpec((tm, ATTN_WIDTH), rows),
            pl.BlockSpec((tm // MOBA_BLOCK, 1, ATTN_WIDTH), lambda i: (i, 0, 0)),
        ],
        out_shape=[
            jax.ShapeDtypeStruct((t, D_MODEL), BF16),
            jax.ShapeDtypeStruct((t, D_MODEL), BF16),
            jax.ShapeDtypeStruct((t, ATTN_WIDTH), BF16),
            jax.ShapeDtypeStruct((t, ATTN_WIDTH), BF16),
            jax.ShapeDtypeStruct((t, ATTN_WIDTH), BF16),
            jax.ShapeDtypeStruct((t // MOBA_BLOCK, 1, ATTN_WIDTH), F32),
        ],
        scratch_shapes=[pltpu.VMEM((8, CONV_WIDTH), F32)],
        compiler_params=pltpu.CompilerParams(
            dimension_semantics=("arbitrary",), vmem_limit_bytes=VMEM_LIMIT_BYTES),
    )(x2, wc, wqkv, wg, bg, wconv, wco, cos, sinlo, sinhi)


def _attn_kernel(q_ref, k_ref, v_ref, kmean_ref, o_ref, bias_ref, m_ref, l_ref, acc_ref, *, n_blocks):
    j = pl.program_id(1)
    blk_rows = MOBA_BLOCK
    scale = HEAD_DIM ** -0.5
    key_i = lax.broadcasted_iota(I32, (blk_rows, blk_rows), 0)
    qry_i = lax.broadcasted_iota(I32, (blk_rows, blk_rows), 1)
    causal = key_i <= qry_i
    blk_i = lax.broadcasted_iota(I32, (n_blocks, blk_rows), 0)
    past = blk_i < j

    row0 = pl.multiple_of(j * blk_rows, blk_rows)
    for h in range(N_HEADS):
        cols = slice(h * HEAD_DIM, (h + 1) * HEAD_DIM)
        qh = q_ref[:, cols]
        km = kmean_ref[:, 0, cols]
        gate = _dot_nt(km, qh.astype(F32))
        gate = jnp.where(past, gate, -jnp.inf)
        rank = jnp.zeros(gate.shape, F32)
        for m in range(n_blocks):
            gm = gate[m:m + 1, :]
            ahead = jnp.where(gm > gate, 1.0, jnp.where(gm == gate, jnp.where(blk_i > m, 1.0, 0.0), 0.0))
            rank = rank + ahead
        bias_ref[h] = jnp.where(past, jnp.where(rank < MOBA_TOPK, 0.0, NEG_BIG), NEG_BIG)

        kj = k_ref[pl.ds(row0, blk_rows), cols]
        vj = v_ref[pl.ds(row0, blk_rows), cols]
        s = _dot_nt(kj, qh) * scale
        s = jnp.where(causal, s, NEG_BIG)
        m0 = jnp.max(s, axis=0, keepdims=True)
        p = jnp.exp(s - m0)
        m_ref[h:h + 1, :] = m0
        l_ref[h:h + 1, :] = jnp.sum(p, axis=0, keepdims=True)
        acc_ref[cols, :] = _dot_tn(vj, p.astype(vj.dtype))

    def body(n, carry):
        r = pl.multiple_of(n * blk_rows, blk_rows)
        for h in range(N_HEADS):
            cols = slice(h * HEAD_DIM, (h + 1) * HEAD_DIM)
            qh = q_ref[:, cols]
            kn = k_ref[pl.ds(r, blk_rows), cols]
            vn = v_ref[pl.ds(r, blk_rows), cols]
            sn = _dot_nt(kn, qh) * scale + bias_ref[h, pl.ds(n, 1), :]
            m_run = m_ref[h:h + 1, :]
            m_new = jnp.maximum(m_run, jnp.max(sn, axis=0, keepdims=True))
            alpha = jnp.exp(m_run - m_new)
            pn = jnp.exp(sn - m_new)
            m_ref[h:h + 1, :] = m_new
            l_ref[h:h + 1, :] = alpha * l_ref[h:h + 1, :] + jnp.sum(pn, axis=0, keepdims=True)
            acc_ref[cols, :] = acc_ref[cols, :] * alpha + _dot_tn(vn, pn.astype(vn.dtype))
        return carry

    lax.fori_loop(0, j, body, 0)
    for h in range(N_HEADS):
        cols = slice(h * HEAD_DIM, (h + 1) * HEAD_DIM)
        acc_ref[cols, :] = acc_ref[cols, :] / l_ref[h:h + 1, :]
    o_ref[...] = acc_ref[...].T.astype(o_ref.dtype)


def _attention(q, k, v, kmean, batch, seq):
    n_blocks = seq // MOBA_BLOCK
    return pl.pallas_call(
        functools.partial(_attn_kernel, n_blocks=n_blocks),
        name="attention",
        grid=(batch, n_blocks),
        in_specs=[
            pl.BlockSpec((MOBA_BLOCK, ATTN_WIDTH), lambda b, j: (b * n_blocks + j, 0)),
            pl.BlockSpec((seq, ATTN_WIDTH), lambda b, j: (b, 0)),
            pl.BlockSpec((seq, ATTN_WIDTH), lambda b, j: (b, 0)),
            pl.BlockSpec((n_blocks, 1, ATTN_WIDTH), lambda b, j: (b, 0, 0)),
        ],
        out_specs=pl.BlockSpec((MOBA_BLOCK, ATTN_WIDTH), lambda b, j: (b * n_blocks + j, 0)),
        out_shape=jax.ShapeDtypeStruct(q.shape, BF16),
        scratch_shapes=[
            pltpu.VMEM((N_HEADS, n_blocks, MOBA_BLOCK), F32),
            pltpu.VMEM((N_HEADS, MOBA_BLOCK), F32),
            pltpu.VMEM((N_HEADS, MOBA_BLOCK), F32),
            pltpu.VMEM((ATTN_WIDTH, MOBA_BLOCK), F32),
        ],
        compiler_params=pltpu.CompilerParams(
            dimension_semantics=("arbitrary", "arbitrary"), vmem_limit_bytes=VMEM_LIMIT_BYTES),
    )(q, k, v, kmean)


def _post_kernel(o_ref, a_ref, gattn_ref, x_ref, wao_ref, wmo_ref, g1_ref, b1_ref,
                 ws1_ref, ws3_ref, ws2_ref, wrt_ref,
                 base_ref, ht_ref, logit_ref):
    y_attn = _dot(o_ref[...], wao_ref[...])
    merged = a_ref[...].astype(F32) + gattn_ref[...].astype(F32) * y_attn
    mix = _dot(merged.astype(BF16), wmo_ref[...])
    h = _layer_norm(DN_ALPHA * x_ref[...] + mix, g1_ref[...], b1_ref[...])
    hb = h.astype(BF16)
    s1 = _dot(hb, ws1_ref[...])
    s3 = _dot(hb, ws3_ref[...])
    shared = _dot((s1 * _sigmoid(s1) * s3).astype(BF16), ws2_ref[...])
    base_ref[...] = DN_ALPHA * h + shared
    _store_row_tiles(ht_ref, h)
    logit_ref[...] = _dot_nt(wrt_ref[...], h)


def _post(o, a, gattn, x2, wao, wmo, g1, b1, ws1, ws3, ws2, wrt):
    t = x2.shape[0]
    tm = PROJ_ROWS
    const = lambda i: (0, 0)
    rows = lambda i: (i, 0)
    return pl.pallas_call(
        _post_kernel,
        name="post",
        grid=(t // tm,),
        in_specs=[
            pl.BlockSpec((tm, ATTN_WIDTH), rows),
            pl.BlockSpec((tm, D_MODEL), rows),
            pl.BlockSpec((tm, D_MODEL), rows),
            pl.BlockSpec((tm, D_MODEL), rows),
            pl.BlockSpec(wao.shape, const),
            pl.BlockSpec(wmo.shape, const),
            pl.BlockSpec(g1.shape, const),
            pl.BlockSpec(b1.shape, const),
            pl.BlockSpec(ws1.shape, const),
            pl.BlockSpec(ws3.shape, const),
            pl.BlockSpec(ws2.shape, const),
            pl.BlockSpec(wrt.shape, const),
        ],
        out_specs=[
            pl.BlockSpec((tm, D_MODEL), rows),
            pl.BlockSpec((tm * ROW_TILE, LANES), rows),
            pl.BlockSpec((N_EXPERTS, tm), lambda i: (0, i)),
        ],
        out_shape=[
            jax.ShapeDtypeStruct((t, D_MODEL), F32),
            jax.ShapeDtypeStruct((t * ROW_TILE, LANES), F32),
            jax.ShapeDtypeStruct((N_EXPERTS, t), F32),
        ],
        compiler_params=pltpu.CompilerParams(
            dimension_semantics=("arbitrary",), vmem_limit_bytes=VMEM_LIMIT_BYTES),
    )(o, a, gattn, x2, wao, wmo, g1, b1, ws1, ws3, ws2, wrt)


def _route_kernel(logit_ref, bias_ref, tri_ref, eid_ref, pos_ref, wt_ref, cnt_ref, run_ref):
    i = pl.program_id(0)
    tm = logit_ref.shape[1]
    per_group = N_EXPERTS // N_GROUPS

    @pl.when(i == 0)
    def _():
        run_ref[...] = jnp.zeros(run_ref.shape, F32)

    scores = _sigmoid(logit_ref[...])
    biased = scores + bias_ref[...]
    b3 = biased.reshape(N_GROUPS, per_group, tm)
    i3 = lax.broadcasted_iota(I32, b3.shape, 1)
    m1 = jnp.max(b3, axis=1)
    first = jnp.min(jnp.where(b3 == m1[:, None, :], i3, per_group), axis=1)
    m2 = jnp.max(jnp.where(i3 == first[:, None, :], -jnp.inf, b3), axis=1)
    gscore = m1 + m2
    g_i = lax.broadcasted_iota(I32, gscore.shape, 0)
    grank = jnp.zeros(gscore.shape, F32)
    for m in range(N_GROUPS):
        gm = gscore[m:m + 1, :]
        ahead = jnp.where(gm > gscore, 1.0, jnp.where(gm == gscore, jnp.where(g_i > m, 1.0, 0.0), 0.0))
        grank = grank + ahead
    gsel = grank < TOPK_GROUPS
    work = jnp.where(gsel[:, None, :], b3, -jnp.inf).reshape(N_EXPERTS, tm)

    e_i = lax.broadcasted_iota(I32, (N_EXPERTS, tm), 0)
    sel = jnp.zeros((N_EXPERTS, tm), F32)
    idxs = []
    scs = []
    for _k in range(TOP_K):
        mx = jnp.max(work, axis=0, keepdims=True)
        idx = jnp.min(jnp.where(work == mx, e_i, N_EXPERTS), axis=0, keepdims=True)
        hit = e_i == idx
        scs.append(jnp.sum(jnp.where(hit, scores, 0.0), axis=0, keepdims=True))
        sel = sel + jnp.where(hit, 1.0, 0.0)
        work = jnp.where(hit, -jnp.inf, work)
        idxs.append(idx)

    rank = _dot(sel.astype(BF16), tri_ref[...]) + run_ref[...]
    run_ref[...] = run_ref[...] + jnp.sum(sel, axis=1, keepdims=True)
    cnt_ref[...] = run_ref[...]

    wsum = scs[0]
    for kk in range(1, TOP_K):
        wsum = wsum + scs[kk]
    for kk in range(TOP_K):
        pos = jnp.sum(jnp.where(e_i == idxs[kk], rank, 0.0), axis=0, keepdims=True)
        eid_ref[kk:kk + 1, :] = idxs[kk]
        pos_ref[kk:kk + 1, :] = pos.astype(I32)
        wt_ref[kk:kk + 1, :] = scs[kk] / wsum * ROUTED_SCALE


def _route(logits_t, bias_col, tri):
    t = logits_t.shape[1]
    tm = ROUTE_COLS
    cols = lambda i: (0, i)
    const = lambda i: (0, 0)
    return pl.pallas_call(
        _route_kernel,
        name="route",
        grid=(t // tm,),
        in_specs=[
            pl.BlockSpec((N_EXPERTS, tm), cols),
            pl.BlockSpec((N_EXPERTS, 1), const),
            pl.BlockSpec((tm, tm), const),
        ],
        out_specs=[
            pl.BlockSpec((TOP_K, tm), cols),
            pl.BlockSpec((TOP_K, tm), cols),
            pl.BlockSpec((TOP_K, tm), cols),
            pl.BlockSpec((N_EXPERTS, 1), const),
        ],
        out_shape=[
            jax.ShapeDtypeStruct((TOP_K, t), I32),
            jax.ShapeDtypeStruct((TOP_K, t), I32),
            jax.ShapeDtypeStruct((TOP_K, t), F32),
            jax.ShapeDtypeStruct((N_EXPERTS, 1), F32),
        ],
        scratch_shapes=[pltpu.VMEM((N_EXPERTS, 1), F32)],
        compiler_params=pltpu.CompilerParams(
            dimension_semantics=("arbitrary",), vmem_limit_bytes=VMEM_LIMIT_BYTES),
    )(logits_t, bias_col, tri)


def _slots_kernel(eid_ref, pos_ref, pstart_ref, dest_ref):
    tm = eid_ref.shape[1]
    e_i = lax.broadcasted_iota(I32, (N_EXPERTS, tm), 0)
    pstart = pstart_ref[...]
    for kk in range(TOP_K):
        start = jnp.sum(jnp.where(e_i == eid_ref[kk:kk + 1, :], pstart, 0.0), axis=0, keepdims=True)
        dest_ref[kk:kk + 1, :] = start.astype(I32) + pos_ref[kk:kk + 1, :]


def _slots(eid, pos, pstart_col):
    t = eid.shape[1]
    tm = ROUTE_COLS
    cols = lambda i: (0, i)
    return pl.pallas_call(
        _slots_kernel,
        name="slots",
        grid=(t // tm,),
        in_specs=[
            pl.BlockSpec((TOP_K, tm), cols),
            pl.BlockSpec((TOP_K, tm), cols),
            pl.BlockSpec((N_EXPERTS, 1), lambda i: (0, 0)),
        ],
        out_specs=pl.BlockSpec((TOP_K, tm), cols),
        out_shape=jax.ShapeDtypeStruct((TOP_K, t), I32),
        compiler_params=pltpu.CompilerParams(dimension_semantics=("arbitrary",)),
    )(eid, pos, pstart_col)


def _row_tile_at(ref, row):
    return ref.at[pl.ds(pl.multiple_of(row * ROW_TILE, ROW_TILE), ROW_TILE)]


def _dispatch_kernel(dest_ref, pad_lo_ref, pad_hi_ref, nvalid_ref, ht_ref, xpad_ref,
                     stage_ref, zero_ref, sem, pad_sem, tail_sem, *, experts_per_step):
    i = pl.program_id(0)
    n = pl.num_programs(0)
    tm = DISPATCH_ROWS
    slot = i % 2

    def pad_range(step, q):
        e = step * experts_per_step + q
        ec = jnp.minimum(e, N_EXPERTS - 1)
        lo = pad_lo_ref[ec]
        return lo, jnp.where(e < N_EXPERTS, pad_hi_ref[ec], lo)

    def start_pad(step):
        for q in range(experts_per_step):
            lo, hi = pad_range(step, q)

            def start_zero(s, c):
                pltpu.make_async_copy(
                    _row_tile_at(zero_ref, s - lo), _row_tile_at(xpad_ref, s), pad_sem).start()
                return c

            lax.fori_loop(lo, hi, start_zero, 0)

    def wait_pad(step):
        for q in range(experts_per_step):
            lo, hi = pad_range(step, q)
            count = hi - lo
            p = EXPERT_ROWS // 2
            while p >= 1:
                @pl.when((count & p) != 0)
                def _(p=p):
                    pltpu.make_async_copy(
                        zero_ref.at[pl.ds(0, p * ROW_TILE)], xpad_ref.at[pl.ds(0, p * ROW_TILE)], pad_sem).wait()
                p //= 2

    def tail_copy(tile):
        rows = EXPERT_ROWS * ROW_TILE
        return pltpu.make_async_copy(
            zero_ref, xpad_ref.at[pl.ds(pl.multiple_of(tile * rows, rows), rows)], tail_sem)

    def start_tail(tile, c):
        tail_copy(tile).start()
        return c

    def wait_tail(tile, c):
        tail_copy(tile).wait()
        return c

    n_tiles = xpad_ref.shape[0] // (EXPERT_ROWS * ROW_TILE)

    @pl.when(i == 0)
    def _():
        zero_ref[...] = jnp.zeros(zero_ref.shape, zero_ref.dtype)
        lax.fori_loop(nvalid_ref[0], n_tiles, start_tail, 0)

    start_pad(i)

    stage_ref[slot] = ht_ref[...]

    def issue(r, carry):
        src = _row_tile_at(stage_ref.at[slot], r)
        for kk in range(TOP_K):
            pltpu.make_async_copy(
                src, _row_tile_at(xpad_ref, dest_ref[r * TOP_K + kk]), sem.at[slot]).start(priority=kk % 2)
        return carry

    lax.fori_loop(0, tm, issue, 0)

    def drain(s):
        for _kk in range(TOP_K):
            pltpu.make_async_copy(
                stage_ref.at[s], xpad_ref.at[pl.ds(0, tm * ROW_TILE)], sem.at[s]).wait()

    @pl.when(i > 0)
    def _():
        drain(1 - slot)
        wait_pad(i - 1)

    @pl.when(i == n - 1)
    def _():
        drain(slot)
        wait_pad(i)
        lax.fori_loop(nvalid_ref[0], n_tiles, wait_tail, 0)


def _dispatch(dest_flat, pad_lo, pad_hi, n_valid, ht, cap):
    t = ht.shape[0] // ROW_TILE
    tm = DISPATCH_ROWS
    experts_per_step = -(-N_EXPERTS // (t // tm))
    return pl.pallas_call(
        functools.partial(_dispatch_kernel, experts_per_step=experts_per_step),
        name="dispatch",
        grid=(t // tm,),
        in_specs=[
            pl.BlockSpec((tm * TOP_K,), lambda i: (i,), memory_space=pltpu.SMEM),
            pl.BlockSpec(memory_space=pltpu.SMEM),
            pl.BlockSpec(memory_space=pltpu.SMEM),
            pl.BlockSpec(memory_space=pltpu.SMEM),
            pl.BlockSpec((tm * ROW_TILE, LANES), lambda i: (i, 0)),
        ],
        out_specs=pl.BlockSpec(memory_space=pl.ANY),
        out_shape=jax.ShapeDtypeStruct((cap * ROW_TILE, LANES), F32),
        scratch_shapes=[
            pltpu.VMEM((2, tm * ROW_TILE, LANES), F32),
            pltpu.VMEM((EXPERT_ROWS * ROW_TILE, LANES), F32),
            pltpu.SemaphoreType.DMA((2,)),
            pltpu.SemaphoreType.DMA(()),
            pltpu.SemaphoreType.DMA(()),
        ],
        compiler_params=pltpu.CompilerParams(
            dimension_semantics=("arbitrary",), has_side_effects=True),
    )(dest_flat, pad_lo, pad_hi, n_valid, ht)


def _expert_kernel(texp_ref, nvalid_ref, first_ref, next_ref, par_ref,
                   x_ref, w1_hbm, w3_hbm, w2_hbm, y_ref,
                   w1f_ref, w3f_ref, w2f_ref, w1b_ref, w3b_ref, w2b_ref, sem):
    i = pl.program_id(0)
    tm = EXPERT_ROWS
    valid = i < nvalid_ref[0]

    def weight_copies(e, s):
        return (pltpu.make_async_copy(w1_hbm.at[e], w1f_ref.at[s], sem.at[s]),
                pltpu.make_async_copy(w3_hbm.at[e], w3f_ref.at[s], sem.at[s]),
                pltpu.make_async_copy(w2_hbm.at[e], w2f_ref.at[s], sem.at[s]))

    @pl.when(jnp.logical_and(valid, first_ref[i] == 1))
    def _():
        s = par_ref[i]

        @pl.when(i == 0)
        def _():
            for c in weight_copies(texp_ref[0], 0):
                c.start()

        for c in weight_copies(texp_ref[i], s):
            c.wait()
        nxt = next_ref[i]

        @pl.when(nxt >= 0)
        def _():
            for c in weight_copies(nxt, 1 - s):
                c.start()

        w1b_ref[...] = w1f_ref[s].astype(BF16)
        w3b_ref[...] = w3f_ref[s].astype(BF16)
        w2b_ref[...] = w2f_ref[s].astype(BF16)

    @pl.when(valid)
    def _():
        xb = _load_row_tiles(x_ref, tm, BF16)
        g = _dot(xb, w1b_ref[...])
        u = _dot(xb, w3b_ref[...])
        hid = (g * _sigmoid(g) * u).astype(BF16)
        _store_row_tiles(y_ref, _dot(hid, w2b_ref[...]))

    @pl.when(jnp.logical_not(valid))
    def _():
        y_ref[...] = jnp.zeros(y_ref.shape, y_ref.dtype)


def _experts(tile_expert, n_valid, tile_first, tile_next, tile_par, xpad, w1, w3, w2):
    tm = EXPERT_ROWS
    n_tiles = xpad.shape[0] // (tm * ROW_TILE)

    def x_map(i, te, nv, fi, nx, pa):
        return (jnp.minimum(i, jnp.maximum(nv[0] - 1, 0)), 0)

    grid_spec = pltpu.PrefetchScalarGridSpec(
        num_scalar_prefetch=5,
        grid=(n_tiles,),
        in_specs=[
            pl.BlockSpec((tm * ROW_TILE, LANES), x_map),
            pl.BlockSpec(memory_space=pl.ANY),
            pl.BlockSpec(memory_space=pl.ANY),
            pl.BlockSpec(memory_space=pl.ANY),
        ],
        out_specs=pl.BlockSpec((tm * ROW_TILE, LANES), lambda i, te, nv, fi, nx, pa: (i, 0)),
        scratch_shapes=[
            pltpu.VMEM((2, D_MODEL, EXPERT_DIM), F32),
            pltpu.VMEM((2, D_MODEL, EXPERT_DIM), F32),
            pltpu.VMEM((2, EXPERT_DIM, D_MODEL), F32),
            pltpu.VMEM((D_MODEL, EXPERT_DIM), BF16),
            pltpu.VMEM((D_MODEL, EXPERT_DIM), BF16),
            pltpu.VMEM((EXPERT_DIM, D_MODEL), BF16),
            pltpu.SemaphoreType.DMA((2,)),
        ],
    )
    return pl.pallas_call(
        _expert_kernel,
        name="experts",
        grid_spec=grid_spec,
        out_shape=jax.ShapeDtypeStruct(xpad.shape, F32),
        compiler_params=pltpu.CompilerParams(
            dimension_semantics=("arbitrary",), vmem_limit_bytes=VMEM_LIMIT_BYTES),
    )(tile_expert, n_valid, tile_first, tile_next, tile_par, xpad, w1, w3, w2)


def _combine_kernel(dest_ref, dest_next_ref, wt_ref, base_ref, g2_ref, b2_ref, ypad_ref, out_ref, buf_ref, sem):
    i = pl.program_id(0)
    n = pl.num_programs(0)
    tm = COMBINE_ROWS
    slot = i % 2

    def issue_from(d_ref, s):
        def issue(r, carry):
            for kk in range(TOP_K):
                pltpu.make_async_copy(
                    _row_tile_at(ypad_ref, d_ref[r * TOP_K + kk]), _row_tile_at(buf_ref.at[s, kk], r),
                    sem.at[s]).start(priority=kk % 2)
            return carry
        lax.fori_loop(0, tm, issue, 0)

    @pl.when(i == 0)
    def _():
        issue_from(dest_ref, 0)

    @pl.when(i + 1 < n)
    def _():
        issue_from(dest_next_ref, 1 - slot)

    for kk in range(TOP_K):
        pltpu.make_async_copy(
            ypad_ref.at[pl.ds(0, tm * ROW_TILE)], buf_ref.at[slot, kk], sem.at[slot]).wait()

    wt = wt_ref[...]
    wks = [wt[:, kk:kk + 1] for kk in range(TOP_K)]
    parts = []
    for s in range(ROW_TILE):
        acc = base_ref[:, s * LANES:(s + 1) * LANES]
        for kk in range(TOP_K):
            acc = acc + wks[kk] * buf_ref[slot, kk, pl.ds(s, tm, stride=ROW_TILE), :]
        parts.append(acc)
    r = jnp.concatenate(parts, axis=1)
    out_ref[...] = _layer_norm(r, g2_ref[...], b2_ref[...])


def _combine(dest_flat, wt_tok, base, g2, b2, ypad):
    t = base.shape[0]
    tm = COMBINE_ROWS
    n = t // tm
    rows = lambda i: (i, 0)
    const = lambda i: (0, 0)
    return pl.pallas_call(
        _combine_kernel,
        name="combine",
        grid=(n,),
        in_specs=[
            pl.BlockSpec((tm * TOP_K,), lambda i: (i,), memory_space=pltpu.SMEM),
            pl.BlockSpec((tm * TOP_K,), lambda i: (jnp.minimum(i + 1, n - 1),), memory_space=pltpu.SMEM),
            pl.BlockSpec((tm, TOP_K), rows),
            pl.BlockSpec((tm, D_MODEL), rows),
            pl.BlockSpec(g2.shape, const),
            pl.BlockSpec(b2.shape, const),
            pl.BlockSpec(memory_space=pl.ANY),
        ],
        out_specs=pl.BlockSpec((tm, D_MODEL), rows),
        out_shape=jax.ShapeDtypeStruct((t, D_MODEL), F32),
        scratch_shapes=[
            pltpu.VMEM((2, TOP_K, tm * ROW_TILE, LANES), F32),
            pltpu.SemaphoreType.DMA((2,)),
        ],
        compiler_params=pltpu.CompilerParams(
            dimension_semantics=("arbitrary",), vmem_limit_bytes=VMEM_LIMIT_BYTES),
    )(dest_flat, dest_flat, wt_tok, base, g2, b2, ypad)


def _rope_tables(seq):
    half = ROPE_DIM // 2
    inv = jnp.power(ROPE_THETA, -jnp.arange(half, dtype=F32) * 2.0 / ROPE_DIM)
    ang = jnp.arange(seq, dtype=F32)[:, None] * inv[None, :]
    cos = jnp.cos(ang)
    sin = jnp.sin(ang)
    ones = jnp.ones((seq, HEAD_DIM - ROPE_DIM), F32)
    zeros_rest = jnp.zeros((seq, HEAD_DIM - ROPE_DIM), F32)
    zeros_half = jnp.zeros((seq, half), F32)
    cos_h = jnp.concatenate([cos, cos, ones], axis=1)
    sinlo_h = jnp.concatenate([-sin, zeros_half, zeros_rest], axis=1)
    sinhi_h = jnp.concatenate([zeros_half, sin, zeros_rest], axis=1)
    tile = lambda m: jnp.tile(m, (1, N_HEADS))
    return tile(cos_h), tile(sinlo_h), tile(sinhi_h)


def kernel(x, w_in, b_gate, w_conv, w_conv_out, w_attn_out, w_mix_out, ln1_g, ln1_b, w_router, router_bias,
           w1, w3, w2, ws1, ws3, ws2, ln2_g, ln2_b):
    batch, seq, d = x.shape
    t = batch * seq
    assert d == D_MODEL and seq % PROJ_ROWS == 0 and seq % MOBA_BLOCK == 0
    x2 = x.reshape(t, d)
    h = None
    for layer in range(DEPTH):
        c3 = 3 * CONV_WIDTH
        a3 = 3 * ATTN_WIDTH
        w_in_b = w_in[layer].astype(BF16)
        wc = w_in_b[:, :c3]
        wqkv = w_in_b[:, c3:c3 + a3]
        wg = w_in_b[:, c3 + a3:]
        cos, sinlo, sinhi = _rope_tables(seq)
        a, gattn, q, k, v, kmean = _proj(
            x2, wc, wqkv, wg, b_gate[layer][None, :], w_conv[layer], w_conv_out[layer].astype(BF16),
            cos, sinlo, sinhi, seq)
        o = _attention(q, k, v, kmean, batch, seq)
        base, ht, logits_t = _post(
            o, a, gattn, x2, w_attn_out[layer].astype(BF16), w_mix_out[layer].astype(BF16),
            ln1_g[layer][None, :], ln1_b[layer][None, :],
            ws1[layer].astype(BF16), ws3[layer].astype(BF16), ws2[layer].astype(BF16),
            w_router[layer].T)

        tri = (jnp.arange(ROUTE_COLS)[:, None] < jnp.arange(ROUTE_COLS)[None, :]).astype(BF16)
        eid, pos, wts, counts = _route(logits_t, router_bias[layer][:, None], tri)

        cnt = counts[:, 0].astype(I32)
        padded = (cnt + EXPERT_ROWS - 1) // EXPERT_ROWS * EXPERT_ROWS
        pend = jnp.cumsum(padded)
        pstart = pend - padded
        cap = t * TOP_K + N_EXPERTS * EXPERT_ROWS
        n_tiles = cap // EXPERT_ROWS
        dest = _slots(eid, pos, pstart.astype(F32)[:, None])
        dest_flat = dest.T.reshape(t * TOP_K)
        tile_start = jnp.arange(n_tiles, dtype=I32) * EXPERT_ROWS
        tile_expert = jnp.minimum(
            jnp.sum((pend[None, :] <= tile_start[:, None]).astype(I32), axis=1), N_EXPERTS - 1)
        n_valid = (pend[-1:] // EXPERT_ROWS).astype(I32)
        tile_i = jnp.arange(n_tiles, dtype=I32)
        prev_expert = jnp.concatenate([jnp.full((1,), -1, I32), tile_expert[:-1]])
        tile_first = ((tile_i < n_valid[0]) & (tile_expert != prev_expert)).astype(I32)
        first_pos = jnp.where(tile_first == 1, tile_i, n_tiles)
        later_first = lax.cummin(jnp.concatenate([first_pos[1:], jnp.full((1,), n_tiles, I32)]), reverse=True)
        tile_next = jnp.where(later_first < n_tiles, tile_expert[jnp.minimum(later_first, n_tiles - 1)], -1)
        tile_par = (jnp.cumsum(tile_first) - 1) % 2

        xpad = _dispatch(dest_flat, pstart + cnt, pend, n_valid, ht, cap)
        ypad = _experts(tile_expert, n_valid, tile_first, tile_next.astype(I32), tile_par.astype(I32),
                        xpad, w1[layer], w3[layer], w2[layer])
        h = _combine(dest_flat, wts.T, base, ln2_g[layer][None, :], ln2_b[layer][None, :], ypad)
        x2 = h
    return h.reshape(batch, seq, d)
```

```python
import functools

import jax
import jax.numpy as jnp
from jax import lax
from jax.experimental import pallas as pl
from jax.experimental.pallas import tpu as pltpu

D_MODEL = 1024
CONV_WIDTH = 1024
CONV_K = 3
N_HEADS = 8
HEAD_DIM = 64
ATTN_WIDTH = N_HEADS * HEAD_DIM
MOBA_BLOCK = 256
MOBA_TOPK = 3
ROPE_THETA = 500000.0
ROPE_DIM = HEAD_DIM // 4
N_EXPERTS = 256
TOP_K = 8
N_GROUPS = 8
TOPK_GROUPS = 4
EXPERT_DIM = 256
SHARED_DIM = 256
ROUTED_SCALE = 2.5
DEPTH = 1
DN_ALPHA = (2 * DEPTH) ** 0.25
LN_EPS = 1e-5

NEG_BIG = -1e30
QK_SCALE = HEAD_DIM ** -0.5
assert QK_SCALE == 2.0 ** -3

VMEM_LIMIT_BYTES = 56 * 1024 * 1024

PROJ_ROWS = 512
ROUTE_COLS = 256
EXPERT_ROWS = 256
DISPATCH_ROWS = 256
COMBINE_ROWS = 256

F32 = jnp.float32
BF16 = jnp.bfloat16
U32 = jnp.uint32
I32 = jnp.int32


def _sigmoid(v):
    return 1.0 / (1.0 + jnp.exp(-v))


def _dot(a, b):
    return jnp.dot(a, b, preferred_element_type=F32)


def _dot_nt(a, b):
    return lax.dot_general(a, b, (((1,), (1,)), ((), ())), preferred_element_type=F32)


def _dot_tn(a, b):
    return lax.dot_general(a, b, (((0,), (0,)), ((), ())), preferred_element_type=F32)


ROW_TILE = 8
LANES = 128


def _store_row_tiles(ref, val):
    m = val.shape[0]
    for s in range(ROW_TILE):
        ref[pl.ds(s, m, stride=ROW_TILE), :] = val[:, s * LANES:(s + 1) * LANES]


def _load_row_tiles(ref, m, dtype=None):
    parts = [ref[pl.ds(s, m, stride=ROW_TILE), :] for s in range(ROW_TILE)]
    if dtype is not None:
        parts = [p.astype(dtype) for p in parts]
    return jnp.concatenate(parts, axis=1)


def _layer_norm(r, g, b):
    mu = jnp.mean(r, axis=-1, keepdims=True)
    c = r - mu
    var = jnp.mean(c * c, axis=-1, keepdims=True)
    return c * lax.rsqrt(var + LN_EPS) * g + b


def _proj_kernel(x_ref, wc_ref, wqkv_ref, wg_ref, bg_ref, wconv_ref, wco_ref,
                 cos_ref, sinlo_ref, sinhi_ref,
                 a_ref, gattn_ref, q_ref, k_ref, vt_ref, kmean_ref,
                 halo_ref, *, tiles_per_seq):
    tm = x_ref.shape[0]
    i = pl.program_id(0)
    xb = x_ref[...].astype(BF16)

    cb = _dot(xb, wc_ref[:, 0:CONV_WIDTH])
    cc = _dot(xb, wc_ref[:, CONV_WIDTH:2 * CONV_WIDTH])
    cv = _dot(xb, wc_ref[:, 2 * CONV_WIDTH:3 * CONV_WIDTH])
    u = cc * cv
    @pl.when((i % tiles_per_seq) == 0)
    def _():
        halo_ref[...] = jnp.zeros(halo_ref.shape, F32)

    prev = halo_ref[...]
    row = lax.broadcasted_iota(I32, u.shape, 0)
    u1 = jnp.where(row == 0, prev[7:8, :], pltpu.roll(u, 1, 0))
    u2 = jnp.where(row == 0, prev[6:7, :], jnp.where(row == 1, prev[7:8, :], pltpu.roll(u, 2, 0)))
    halo_ref[...] = u[tm - 8:tm, :]
    w0 = wconv_ref[0:1, :]
    w1 = wconv_ref[1:2, :]
    w2 = wconv_ref[2:3, :]
    yc = cb * (w0 * u2 + w1 * u1 + w2 * u)
    y_conv = _dot(yc.astype(BF16), wco_ref[...])

    g = _dot(xb, wg_ref[...]) + bg_ref[...]
    gates = _sigmoid(g)
    a_ref[...] = (gates[:, :D_MODEL] * y_conv).astype(a_ref.dtype)
    gattn_ref[...] = gates[:, D_MODEL:].astype(gattn_ref.dtype)

    qkv = _dot(xb, wqkv_ref[...])
    cos = cos_ref[...]
    sinlo = sinlo_ref[...]
    sinhi = sinhi_ref[...]

    def rope(t):
        return (t * cos + pltpu.roll(t, ATTN_WIDTH - ROPE_DIM // 2, 1) * sinlo
                + pltpu.roll(t, ROPE_DIM // 2, 1) * sinhi)

    q = rope(qkv[:, 0:ATTN_WIDTH])
    k = rope(qkv[:, ATTN_WIDTH:2 * ATTN_WIDTH])
    q_ref[...] = (q * QK_SCALE).astype(q_ref.dtype)
    k_ref[...] = k.astype(k_ref.dtype)
    v = qkv[:, 2 * ATTN_WIDTH:]
    for blk in range(tm // MOBA_BLOCK):
        rows = slice(blk * MOBA_BLOCK, (blk + 1) * MOBA_BLOCK)
        kmean_ref[blk] = jnp.mean(k[rows, :], axis=0, keepdims=True)
        vt_ref[blk] = v[rows, :].T.astype(vt_ref.dtype)


def _proj(x2, wc, wqkv, wg, bg, wconv, wco, cos, sinlo, sinhi, seq):
    t = x2.shape[0]
    tm = PROJ_ROWS
    tiles_per_seq = seq // tm
    const = lambda i: (0, 0)
    rows = lambda i: (i, 0)
    pos = lambda i: (i % tiles_per_seq, 0)
    return pl.pallas_call(
        functools.partial(_proj_kernel, tiles_per_seq=tiles_per_seq),
        name="proj",
        grid=(t // tm,),
        in_specs=[
            pl.BlockSpec((tm, D_MODEL), rows),
            pl.BlockSpec(wc.shape, const),
            pl.BlockSpec(wqkv.shape, const),
            pl.BlockSpec(wg.shape, const),
            pl.BlockSpec(bg.shape, const),
            pl.BlockSpec(wconv.shape, const),
            pl.BlockSpec(wco.shape, const),
            pl.BlockSpec((tm, ATTN_WIDTH), pos),
            pl.BlockSpec((tm, ATTN_WIDTH), pos),
            pl.BlockSpec((tm, ATTN_WIDTH), pos),
        ],
        out_specs=[
            pl.BlockSpec((tm, D_MODEL), rows),
            pl.BlockSpec((tm, D_MODEL), rows),
            pl.BlockSpec((tm, ATTN_WIDTH), rows),
            pl.BlockSpec((tm, ATTN_WIDTH), rows),
            pl.BlockSpec((tm // MOBA_BLOCK, ATTN_WIDTH, MOBA_BLOCK), lambda i: (i, 0, 0)),
            pl.BlockSpec((tm // MOBA_BLOCK, 1, ATTN_WIDTH), lambda i: (i, 0, 0)),
        ],
        out_shape=[
            jax.ShapeDtypeStruct((t, D_MODEL), BF16),
            jax.ShapeDtypeStruct((t, D_MODEL), BF16),
            jax.ShapeDtypeStruct((t, ATTN_WIDTH), BF16),
            jax.ShapeDtypeStruct((t, ATTN_WIDTH), BF16),
            jax.ShapeDtypeStruct((t // MOBA_BLOCK, ATTN_WIDTH, MOBA_BLOCK), BF16),
            jax.ShapeDtypeStruct((t // MOBA_BLOCK, 1, ATTN_WIDTH), F32),
        ],
        scratch_shapes=[pltpu.VMEM((8, CONV_WIDTH), F32)],
        compiler_params=pltpu.CompilerParams(
            dimension_semantics=("arbitrary",), vmem_limit_bytes=VMEM_LIMIT_BYTES),
    )(x2, wc, wqkv, wg, bg, wconv, wco, cos, sinlo, sinhi)


def _attn_kernel(q_ref, k_ref, vt_ref, kmean_ref, o_ref, bias_ref, m_ref, l_ref, acc_ref, *, n_blocks):
    j = pl.program_id(1)
    blk_rows = MOBA_BLOCK
    key_i = lax.broadcasted_iota(I32, (blk_rows, blk_rows), 0)
    qry_i = lax.broadcasted_iota(I32, (blk_rows, blk_rows), 1)
    causal = key_i <= qry_i
    blk_i = lax.broadcasted_iota(I32, (n_blocks, blk_rows), 0)
    past = blk_i < j

    for h in range(N_HEADS):
        cols = slice(h * HEAD_DIM, (h + 1) * HEAD_DIM)
        qh = q_ref[:, cols]
        km = kmean_ref[:, 0, cols]
        gate = _dot_nt(km, qh.astype(F32))
        gate = jnp.where(past, gate, -jnp.inf)
        rank = jnp.zeros(gate.shape, F32)
        for m in range(n_blocks):
            gm = gate[m:m + 1, :]
            ahead = jnp.where(gm > gate, 1.0, jnp.where(gm == gate, jnp.where(blk_i > m, 1.0, 0.0), 0.0))
            rank = rank + ahead
        bias_ref[h] = jnp.where(past, jnp.where(rank < MOBA_TOPK, 0.0, NEG_BIG), NEG_BIG)

    def scores(n):
        r = pl.multiple_of(n * blk_rows, blk_rows)
        return [_dot_nt(k_ref[pl.ds(r, blk_rows), h * HEAD_DIM:(h + 1) * HEAD_DIM],
                        q_ref[:, h * HEAD_DIM:(h + 1) * HEAD_DIM]) for h in range(N_HEADS)]

    for h, s_raw in enumerate(scores(j)):
        cols = slice(h * HEAD_DIM, (h + 1) * HEAD_DIM)
        s = jnp.where(causal, s_raw, NEG_BIG)
        m0 = jnp.max(s, axis=0, keepdims=True)
        p = jnp.exp(s - m0)
        m_ref[h:h + 1, :] = m0
        l_ref[h:h + 1, :] = jnp.sum(p, axis=0, keepdims=True)
        acc_ref[cols, :] = _dot(vt_ref[j, cols, :], p.astype(vt_ref.dtype))

    def body(n, carry):
        for h, s_raw in enumerate(scores(n)):
            cols = slice(h * HEAD_DIM, (h + 1) * HEAD_DIM)
            sn = s_raw + bias_ref[h, pl.ds(n, 1), :]
            m_run = m_ref[h:h + 1, :]
            m_new = jnp.maximum(m_run, jnp.max(sn, axis=0, keepdims=True))
            alpha = jnp.exp(m_run - m_new)
            pn = jnp.exp(sn - m_new)
            m_ref[h:h + 1, :] = m_new
            l_ref[h:h + 1, :] = alpha * l_ref[h:h + 1, :] + jnp.sum(pn, axis=0, keepdims=True)
            acc_ref[cols, :] = acc_ref[cols, :] * alpha + _dot(vt_ref[n, cols, :], pn.astype(vt_ref.dtype))
        return carry

    lax.fori_loop(0, j, body, 0)
    for h in range(N_HEADS):
        cols = slice(h * HEAD_DIM, (h + 1) * HEAD_DIM)
        acc_ref[cols, :] = acc_ref[cols, :] / l_ref[h:h + 1, :]
    o_ref[...] = acc_ref[...].T.astype(o_ref.dtype)


def _attention(q, k, vt, kmean, batch, seq):
    n_blocks = seq // MOBA_BLOCK
    return pl.pallas_call(
        functools.partial(_attn_kernel, n_blocks=n_blocks),
        name="attention",
        grid=(batch, n_blocks),
        in_specs=[
            pl.BlockSpec((MOBA_BLOCK, ATTN_WIDTH), lambda b, j: (b * n_blocks + j, 0)),
            pl.BlockSpec((seq, ATTN_WIDTH), lambda b, j: (b, 0)),
            pl.BlockSpec((n_blocks, ATTN_WIDTH, MOBA_BLOCK), lambda b, j: (b, 0, 0)),
            pl.BlockSpec((n_blocks, 1, ATTN_WIDTH), lambda b, j: (b, 0, 0)),
        ],
        out_specs=pl.BlockSpec((MOBA_BLOCK, ATTN_WIDTH), lambda b, j: (b * n_blocks + j, 0)),
        out_shape=jax.ShapeDtypeStruct(q.shape, BF16),
        scratch_shapes=[
            pltpu.VMEM((N_HEADS, n_blocks, MOBA_BLOCK), F32),
            pltpu.VMEM((N_HEADS, MOBA_BLOCK), F32),
            pltpu.VMEM((N_HEADS, MOBA_BLOCK), F32),
            pltpu.VMEM((ATTN_WIDTH, MOBA_BLOCK), F32),
        ],
        compiler_params=pltpu.CompilerParams(
            dimension_semantics=("arbitrary", "arbitrary"), vmem_limit_bytes=VMEM_LIMIT_BYTES),
    )(q, k, vt, kmean)


def _post_kernel(o_ref, a_ref, gattn_ref, x_ref, wao_ref, wmo_ref, g1_ref, b1_ref,
                 ws1_ref, ws3_ref, ws2_ref, wrt_ref,
                 base_ref, ht_ref, logit_ref):
    y_attn = _dot(o_ref[...], wao_ref[...])
    merged = a_ref[...].astype(F32) + gattn_ref[...].astype(F32) * y_attn
    mix = _dot(merged.astype(BF16), wmo_ref[...])
    h = _layer_norm(DN_ALPHA * x_ref[...] + mix, g1_ref[...], b1_ref[...])
    hb = h.astype(BF16)
    s1 = _dot(hb, ws1_ref[...])
    s3 = _dot(hb, ws3_ref[...])
    shared = _dot((s1 * _sigmoid(s1) * s3).astype(BF16), ws2_ref[...])
    base_ref[...] = DN_ALPHA * h + shared
    _store_row_tiles(ht_ref, h)
    logit_ref[...] = _dot_nt(wrt_ref[...], h)


def _post(o, a, gattn, x2, wao, wmo, g1, b1, ws1, ws3, ws2, wrt):
    t = x2.shape[0]
    tm = PROJ_ROWS
    const = lambda i: (0, 0)
    rows = lambda i: (i, 0)
    return pl.pallas_call(
        _post_kernel,
        name="post",
        grid=(t // tm,),
        in_specs=[
            pl.BlockSpec((tm, ATTN_WIDTH), rows),
            pl.BlockSpec((tm, D_MODEL), rows),
            pl.BlockSpec((tm, D_MODEL), rows),
            pl.BlockSpec((tm, D_MODEL), rows),
            pl.BlockSpec(wao.shape, const),
            pl.BlockSpec(wmo.shape, const),
            pl.BlockSpec(g1.shape, const),
            pl.BlockSpec(b1.shape, const),
            pl.BlockSpec(ws1.shape, const),
            pl.BlockSpec(ws3.shape, const),
            pl.BlockSpec(ws2.shape, const),
            pl.BlockSpec(wrt.shape, const),
        ],
        out_specs=[
            pl.BlockSpec((tm, D_MODEL), rows),
            pl.BlockSpec((tm * ROW_TILE, LANES), rows),
            pl.BlockSpec((N_EXPERTS, tm), lambda i: (0, i)),
        ],
        out_shape=[
            jax.ShapeDtypeStruct((t, D_MODEL), F32),
            jax.ShapeDtypeStruct((t * ROW_TILE, LANES), F32),
            jax.ShapeDtypeStruct((N_EXPERTS, t), F32),
        ],
        compiler_params=pltpu.CompilerParams(
            dimension_semantics=("arbitrary",), vmem_limit_bytes=VMEM_LIMIT_BYTES),
    )(o, a, gattn, x2, wao, wmo, g1, b1, ws1, ws3, ws2, wrt)


def _route_kernel(logit_ref, bias_ref, tri_ref, eid_ref, pos_ref, wt_ref, cnt_ref, run_ref):
    i = pl.program_id(0)
    tm = logit_ref.shape[1]
    per_group = N_EXPERTS // N_GROUPS

    @pl.when(i == 0)
    def _():
        run_ref[...] = jnp.zeros(run_ref.shape, F32)

    scores = _sigmoid(logit_ref[...])
    biased = scores + bias_ref[...]
    b3 = biased.reshape(N_GROUPS, per_group, tm)
    i3 = lax.broadcasted_iota(I32, b3.shape, 1)
    m1 = jnp.max(b3, axis=1)
    first = jnp.min(jnp.where(b3 == m1[:, None, :], i3, per_group), axis=1)
    m2 = jnp.max(jnp.where(i3 == first[:, None, :], -jnp.inf, b3), axis=1)
    gscore = m1 + m2
    g_i = lax.broadcasted_iota(I32, gscore.shape, 0)
    grank = jnp.zeros(gscore.shape, F32)
    for m in range(N_GROUPS):
        gm = gscore[m:m + 1, :]
        ahead = jnp.where(gm > gscore, 1.0, jnp.where(gm == gscore, jnp.where(g_i > m, 1.0, 0.0), 0.0))
        grank = grank + ahead
    gsel = grank < TOPK_GROUPS
    work = jnp.where(gsel[:, None, :], b3, -jnp.inf).reshape(N_EXPERTS, tm)

    e_i = lax.broadcasted_iota(I32, (N_EXPERTS, tm), 0)
    sel = jnp.zeros((N_EXPERTS, tm), F32)
    idxs = []
    scs = []
    for _k in range(TOP_K):
        mx = jnp.max(work, axis=0, keepdims=True)
        idx = jnp.min(jnp.where(work == mx, e_i, N_EXPERTS), axis=0, keepdims=True)
        hit = e_i == idx
        scs.append(jnp.sum(jnp.where(hit, scores, 0.0), axis=0, keepdims=True))
        sel = sel + jnp.where(hit, 1.0, 0.0)
        work = jnp.where(hit, -jnp.inf, work)
        idxs.append(idx)

    rank = _dot(sel.astype(BF16), tri_ref[...]) + run_ref[...]
    run_ref[...] = run_ref[...] + jnp.sum(sel, axis=1, keepdims=True)
    cnt_ref[...] = run_ref[...]

    wsum = scs[0]
    for kk in range(1, TOP_K):
        wsum = wsum + scs[kk]
    for kk in range(TOP_K):
        pos = jnp.sum(jnp.where(e_i == idxs[kk], rank, 0.0), axis=0, keepdims=True)
        eid_ref[kk:kk + 1, :] = idxs[kk]
        pos_ref[kk:kk + 1, :] = pos.astype(I32)
        wt_ref[kk:kk + 1, :] = scs[kk] / wsum * ROUTED_SCALE


def _route(logits_t, bias_col, tri):
    t = logits_t.shape[1]
    tm = ROUTE_COLS
    cols = lambda i: (0, i)
    const = lambda i: (0, 0)
    return pl.pallas_call(
        _route_kernel,
        name="route",
        grid=(t // tm,),
        in_specs=[
            pl.BlockSpec((N_EXPERTS, tm), cols),
            pl.BlockSpec((N_EXPERTS, 1), const),
            pl.BlockSpec((tm, tm), const),
        ],
        out_specs=[
            pl.BlockSpec((TOP_K, tm), cols),
            pl.BlockSpec((TOP_K, tm), cols),
            pl.BlockSpec((TOP_K, tm), cols),
            pl.BlockSpec((N_EXPERTS, 1), const),
        ],
        out_shape=[
            jax.ShapeDtypeStruct((TOP_K, t), I32),
            jax.ShapeDtypeStruct((TOP_K, t), I32),
            jax.ShapeDtypeStruct((TOP_K, t), F32),
            jax.ShapeDtypeStruct((N_EXPERTS, 1), F32),
        ],
        scratch_shapes=[pltpu.VMEM((N_EXPERTS, 1), F32)],
        compiler_params=pltpu.CompilerParams(
            dimension_semantics=("arbitrary",), vmem_limit_bytes=VMEM_LIMIT_BYTES),
    )(logits_t, bias_col, tri)


def _slots_kernel(eid_ref, pos_ref, pstart_ref, dest_ref):
    tm = eid_ref.shape[1]
    e_i = lax.broadcasted_iota(I32, (N_EXPERTS, tm), 0)
    pstart = pstart_ref[...]
    for kk in range(TOP_K):
        start = jnp.sum(jnp.where(e_i == eid_ref[kk:kk + 1, :], pstart, 0.0), axis=0, keepdims=True)
        dest_ref[kk:kk + 1, :] = start.astype(I32) + pos_ref[kk:kk + 1, :]


def _slots(eid, pos, pstart_col):
    t = eid.shape[1]
    tm = ROUTE_COLS
    cols = lambda i: (0, i)
    return pl.pallas_call(
        _slots_kernel,
        name="slots",
        grid=(t // tm,),
        in_specs=[
            pl.BlockSpec((TOP_K, tm), cols),
            pl.BlockSpec((TOP_K, tm), cols),
            pl.BlockSpec((N_EXPERTS, 1), lambda i: (0, 0)),
        ],
        out_specs=pl.BlockSpec((TOP_K, tm), cols),
        out_shape=jax.ShapeDtypeStruct((TOP_K, t), I32),
        compiler_params=pltpu.CompilerParams(dimension_semantics=("arbitrary",)),
    )(eid, pos, pstart_col)


def _row_tile_at(ref, row):
    return ref.at[pl.ds(pl.multiple_of(row * ROW_TILE, ROW_TILE), ROW_TILE)]


def _dispatch_kernel(dest_ref, pad_lo_ref, pad_hi_ref, nvalid_ref, ht_ref, xpad_ref,
                     stage_ref, zero_ref, sem, pad_sem, tail_sem, *, experts_per_step):
    i = pl.program_id(0)
    n = pl.num_programs(0)
    tm = DISPATCH_ROWS
    slot = i % 2

    def pad_range(step, q):
        e = step * experts_per_step + q
        ec = jnp.minimum(e, N_EXPERTS - 1)
        lo = pad_lo_ref[ec]
        return lo, jnp.where(e < N_EXPERTS, pad_hi_ref[ec], lo)

    def start_pad(step):
        for q in range(experts_per_step):
            lo, hi = pad_range(step, q)

            def start_zero(s, c):
                pltpu.make_async_copy(
                    _row_tile_at(zero_ref, s - lo), _row_tile_at(xpad_ref, s), pad_sem).start()
                return c

            lax.fori_loop(lo, hi, start_zero, 0)

    def wait_pad(step):
        for q in range(experts_per_step):
            lo, hi = pad_range(step, q)
            count = hi - lo
            p = EXPERT_ROWS // 2
            while p >= 1:
                @pl.when((count & p) != 0)
                def _(p=p):
                    pltpu.make_async_copy(
                        zero_ref.at[pl.ds(0, p * ROW_TILE)], xpad_ref.at[pl.ds(0, p * ROW_TILE)], pad_sem).wait()
                p //= 2

    def tail_copy(tile):
        rows = EXPERT_ROWS * ROW_TILE
        return pltpu.make_async_copy(
            zero_ref, xpad_ref.at[pl.ds(pl.multiple_of(tile * rows, rows), rows)], tail_sem)

    def start_tail(tile, c):
        tail_copy(tile).start()
        return c

    def wait_tail(tile, c):
        tail_copy(tile).wait()
        return c

    n_tiles = xpad_ref.shape[0] // (EXPERT_ROWS * ROW_TILE)

    @pl.when(i == 0)
    def _():
        zero_ref[...] = jnp.zeros(zero_ref.shape, zero_ref.dtype)
        lax.fori_loop(nvalid_ref[0], n_tiles, start_tail, 0)

    start_pad(i)

    stage_ref[slot] = ht_ref[...]

    def issue(r, carry):
        src = _row_tile_at(stage_ref.at[slot], r)
        for kk in range(TOP_K):
            pltpu.make_async_copy(
                src, _row_tile_at(xpad_ref, dest_ref[r * TOP_K + kk]), sem.at[slot]).start(priority=kk % 2)
        return carry

    lax.fori_loop(0, tm, issue, 0)

    def drain(s):
        for _kk in range(TOP_K):
            pltpu.make_async_copy(
                stage_ref.at[s], xpad_ref.at[pl.ds(0, tm * ROW_TILE)], sem.at[s]).wait()

    @pl.when(i > 0)
    def _():
        drain(1 - slot)
        wait_pad(i - 1)

    @pl.when(i == n - 1)
    def _():
        drain(slot)
        wait_pad(i)
        lax.fori_loop(nvalid_ref[0], n_tiles, wait_tail, 0)


def _dispatch(dest_flat, pad_lo, pad_hi, n_valid, ht, cap):
    t = ht.shape[0] // ROW_TILE
    tm = DISPATCH_ROWS
    experts_per_step = -(-N_EXPERTS // (t // tm))
    return pl.pallas_call(
        functools.partial(_dispatch_kernel, experts_per_step=experts_per_step),
        name="dispatch",
        grid=(t // tm,),
        in_specs=[
            pl.BlockSpec((tm * TOP_K,), lambda i: (i,), memory_space=pltpu.SMEM),
            pl.BlockSpec(memory_space=pltpu.SMEM),
            pl.BlockSpec(memory_space=pltpu.SMEM),
            pl.BlockSpec(memory_space=pltpu.SMEM),
            pl.BlockSpec((tm * ROW_TILE, LANES), lambda i: (i, 0)),
        ],
        out_specs=pl.BlockSpec(memory_space=pl.ANY),
        out_shape=jax.ShapeDtypeStruct((cap * ROW_TILE, LANES), F32),
        scratch_shapes=[
            pltpu.VMEM((2, tm * ROW_TILE, LANES), F32),
            pltpu.VMEM((EXPERT_ROWS * ROW_TILE, LANES), F32),
            pltpu.SemaphoreType.DMA((2,)),
            pltpu.SemaphoreType.DMA(()),
            pltpu.SemaphoreType.DMA(()),
        ],
        compiler_params=pltpu.CompilerParams(
            dimension_semantics=("arbitrary",), has_side_effects=True),
    )(dest_flat, pad_lo, pad_hi, n_valid, ht)


def _expert_kernel(texp_ref, nvalid_ref, first_ref, next_ref, par_ref,
                   x_ref, w1_hbm, w3_hbm, w2_hbm, y_ref,
                   w1f_ref, w3f_ref, w2f_ref, w1b_ref, w3b_ref, w2b_ref, sem):
    i = pl.program_id(0)
    tm = EXPERT_ROWS
    valid = i < nvalid_ref[0]

    def weight_copies(e, s):
        return (pltpu.make_async_copy(w1_hbm.at[e], w1f_ref.at[s], sem.at[s]),
                pltpu.make_async_copy(w3_hbm.at[e], w3f_ref.at[s], sem.at[s]),
                pltpu.make_async_copy(w2_hbm.at[e], w2f_ref.at[s], sem.at[s]))

    @pl.when(jnp.logical_and(valid, first_ref[i] == 1))
    def _():
        s = par_ref[i]

        @pl.when(i == 0)
        def _():
            for c in weight_copies(texp_ref[0], 0):
                c.start()

        for c in weight_copies(texp_ref[i], s):
            c.wait()
        nxt = next_ref[i]

        @pl.when(nxt >= 0)
        def _():
            for c in weight_copies(nxt, 1 - s):
                c.start()

        w1b_ref[...] = w1f_ref[s].astype(BF16)
        w3b_ref[...] = w3f_ref[s].astype(BF16)
        w2b_ref[...] = w2f_ref[s].astype(BF16)

    @pl.when(valid)
    def _():
        xb = _load_row_tiles(x_ref, tm, BF16)
        g = _dot(xb, w1b_ref[...])
        u = _dot(xb, w3b_ref[...])
        hid = (g * _sigmoid(g) * u).astype(BF16)
        _store_row_tiles(y_ref, _dot(hid, w2b_ref[...]))

    @pl.when(jnp.logical_not(valid))
    def _():
        y_ref[...] = jnp.zeros(y_ref.shape, y_ref.dtype)


def _experts(tile_expert, n_valid, tile_first, tile_next, tile_par, xpad, w1, w3, w2):
    tm = EXPERT_ROWS
    n_tiles = xpad.shape[0] // (tm * ROW_TILE)

    def x_map(i, te, nv, fi, nx, pa):
        return (jnp.minimum(i, jnp.maximum(nv[0] - 1, 0)), 0)

    grid_spec = pltpu.PrefetchScalarGridSpec(
        num_scalar_prefetch=5,
        grid=(n_tiles,),
        in_specs=[
            pl.BlockSpec((tm * ROW_TILE, LANES), x_map),
            pl.BlockSpec(memory_space=pl.ANY),
            pl.BlockSpec(memory_space=pl.ANY),
            pl.BlockSpec(memory_space=pl.ANY),
        ],
        out_specs=pl.BlockSpec((tm * ROW_TILE, LANES), lambda i, te, nv, fi, nx, pa: (i, 0)),
        scratch_shapes=[
            pltpu.VMEM((2, D_MODEL, EXPERT_DIM), F32),
            pltpu.VMEM((2, D_MODEL, EXPERT_DIM), F32),
            pltpu.VMEM((2, EXPERT_DIM, D_MODEL), F32),
            pltpu.VMEM((D_MODEL, EXPERT_DIM), BF16),
            pltpu.VMEM((D_MODEL, EXPERT_DIM), BF16),
            pltpu.VMEM((EXPERT_DIM, D_MODEL), BF16),
            pltpu.SemaphoreType.DMA((2,)),
        ],
    )
    return pl.pallas_call(
        _expert_kernel,
        name="experts",
        grid_spec=grid_spec,
        out_shape=jax.ShapeDtypeStruct(xpad.shape, F32),
        compiler_params=pltpu.CompilerParams(
            dimension_semantics=("arbitrary",), vmem_limit_bytes=VMEM_LIMIT_BYTES),
    )(tile_expert, n_valid, tile_first, tile_next, tile_par, xpad, w1, w3, w2)


def _combine_kernel(dest_ref, dest_next_ref, wt_ref, base_ref, g2_ref, b2_ref, ypad_ref, out_ref, buf_ref, sem):
    i = pl.program_id(0)
    n = pl.num_programs(0)
    tm = COMBINE_ROWS
    slot = i % 2

    def issue_from(d_ref, s):
        def issue(r, carry):
            for kk in range(TOP_K):
                pltpu.make_async_copy(
                    _row_tile_at(ypad_ref, d_ref[r * TOP_K + kk]), _row_tile_at(buf_ref.at[s, kk], r),
                    sem.at[s]).start(priority=kk % 2)
            return carry
        lax.fori_loop(0, tm, issue, 0)

    @pl.when(i == 0)
    def _():
        issue_from(dest_ref, 0)

    @pl.when(i + 1 < n)
    def _():
        issue_from(dest_next_ref, 1 - slot)

    for kk in range(TOP_K):
        pltpu.make_async_copy(
            ypad_ref.at[pl.ds(0, tm * ROW_TILE)], buf_ref.at[slot, kk], sem.at[slot]).wait()

    wt = wt_ref[...]
    wks = [wt[:, kk:kk + 1] for kk in range(TOP_K)]
    parts = []
    for s in range(ROW_TILE):
        acc = base_ref[:, s * LANES:(s + 1) * LANES]
        for kk in range(TOP_K):
            acc = acc + wks[kk] * buf_ref[slot, kk, pl.ds(s, tm, stride=ROW_TILE), :]
        parts.append(acc)
    r = jnp.concatenate(parts, axis=1)
    out_ref[...] = _layer_norm(r, g2_ref[...], b2_ref[...])


def _combine(dest_flat, wt_tok, base, g2, b2, ypad):
    t = base.shape[0]
    tm = COMBINE_ROWS
    n = t // tm
    rows = lambda i: (i, 0)
    const = lambda i: (0, 0)
    return pl.pallas_call(
        _combine_kernel,
        name="combine",
        grid=(n,),
        in_specs=[
            pl.BlockSpec((tm * TOP_K,), lambda i: (i,), memory_space=pltpu.SMEM),
            pl.BlockSpec((tm * TOP_K,), lambda i: (jnp.minimum(i + 1, n - 1),), memory_space=pltpu.SMEM),
            pl.BlockSpec((tm, TOP_K), rows),
            pl.BlockSpec((tm, D_MODEL), rows),
            pl.BlockSpec(g2.shape, const),
            pl.BlockSpec(b2.shape, const),
            pl.BlockSpec(memory_space=pl.ANY),
        ],
        out_specs=pl.BlockSpec((tm, D_MODEL), rows),
        out_shape=jax.ShapeDtypeStruct((t, D_MODEL), F32),
        scratch_shapes=[
            pltpu.VMEM((2, TOP_K, tm * ROW_TILE, LANES), F32),
            pltpu.SemaphoreType.DMA((2,)),
        ],
        compiler_params=pltpu.CompilerParams(
            dimension_semantics=("arbitrary",), vmem_limit_bytes=VMEM_LIMIT_BYTES),
    )(dest_flat, dest_flat, wt_tok, base, g2, b2, ypad)


def _rope_tables(seq):
    half = ROPE_DIM // 2
    inv = jnp.power(ROPE_THETA, -jnp.arange(half, dtype=F32) * 2.0 / ROPE_DIM)
    ang = jnp.arange(seq, dtype=F32)[:, None] * inv[None, :]
    cos = jnp.cos(ang)
    sin = jnp.sin(ang)
    ones = jnp.ones((seq, HEAD_DIM - ROPE_DIM), F32)
    zeros_rest = jnp.zeros((seq, HEAD_DIM - ROPE_DIM), F32)
    zeros_half = jnp.zeros((seq, half), F32)
    cos_h = jnp.concatenate([cos, cos, ones], axis=1)
    sinlo_h = jnp.concatenate([-sin, zeros_half, zeros_rest], axis=1)
    sinhi_h = jnp.concatenate([zeros_half, sin, zeros_rest], axis=1)
    tile = lambda m: jnp.tile(m, (1, N_HEADS))
    return tile(cos_h), tile(sinlo_h), tile(sinhi_h)


def kernel(x, w_in, b_gate, w_conv, w_conv_out, w_attn_out, w_mix_out, ln1_g, ln1_b, w_router, router_bias,
           w1, w3, w2, ws1, ws3, ws2, ln2_g, ln2_b):
    batch, seq, d = x.shape
    t = batch * seq
    assert d == D_MODEL and seq % PROJ_ROWS == 0 and seq % MOBA_BLOCK == 0
    x2 = x.reshape(t, d)
    h = None
    for layer in range(DEPTH):
        c3 = 3 * CONV_WIDTH
        a3 = 3 * ATTN_WIDTH
        w_in_b = w_in[layer].astype(BF16)
        wc = w_in_b[:, :c3]
        wqkv = w_in_b[:, c3:c3 + a3]
        wg = w_in_b[:, c3 + a3:]
        cos, sinlo, sinhi = _rope_tables(seq)
        a, gattn, q, k, vt, kmean = _proj(
            x2, wc, wqkv, wg, b_gate[layer][None, :], w_conv[layer], w_conv_out[layer].astype(BF16),
            cos, sinlo, sinhi, seq)
        o = _attention(q, k, vt, kmean, batch, seq)
        base, ht, logits_t = _post(
            o, a, gattn, x2, w_attn_out[layer].astype(BF16), w_mix_out[layer].astype(BF16),
            ln1_g[layer][None, :], ln1_b[layer][None, :],
            ws1[layer].astype(BF16), ws3[layer].astype(BF16), ws2[layer].astype(BF16),
            w_router[layer].T)

        tri = (jnp.arange(ROUTE_COLS)[:, None] < jnp.arange(ROUTE_COLS)[None, :]).astype(BF16)
        eid, pos, wts, counts = _route(logits_t, router_bias[layer][:, None], tri)

        cnt = counts[:, 0].astype(I32)
        padded = (cnt + EXPERT_ROWS - 1) // EXPERT_ROWS * EXPERT_ROWS
        pend = jnp.cumsum(padded)
        pstart = pend - padded
        cap = t * TOP_K + N_EXPERTS * EXPERT_ROWS
        n_tiles = cap // EXPERT_ROWS
        dest = _slots(eid, pos, pstart.astype(F32)[:, None])
        dest_flat = dest.T.reshape(t * TOP_K)
        tile_start = jnp.arange(n_tiles, dtype=I32) * EXPERT_ROWS
        tile_expert = jnp.minimum(
            jnp.sum((pend[None, :] <= tile_start[:, None]).astype(I32), axis=1), N_EXPERTS - 1)
        n_valid = (pend[-1:] // EXPERT_ROWS).astype(I32)
        tile_i = jnp.arange(n_tiles, dtype=I32)
        prev_expert = jnp.concatenate([jnp.full((1,), -1, I32), tile_expert[:-1]])
        tile_first = ((tile_i < n_valid[0]) & (tile_expert != prev_expert)).astype(I32)
        first_pos = jnp.where(tile_first == 1, tile_i, n_tiles)
        later_first = lax.cummin(jnp.concatenate([first_pos[1:], jnp.full((1,), n_tiles, I32)]), reverse=True)
        tile_next = jnp.where(later_first < n_tiles, tile_expert[jnp.minimum(later_first, n_tiles - 1)], -1)
        tile_par = (jnp.cumsum(tile_first) - 1) % 2

        xpad = _dispatch(dest_flat, pstart + cnt, pend, n_valid, ht, cap)
        ypad = _experts(tile_expert, n_valid, tile_first, tile_next.astype(I32), tile_par.astype(I32),
                        xpad, w1[layer], w3[layer], w2[layer])
        h = _combine(dest_flat, wts.T, base, ln2_g[layer][None, :], ln2_b[layer][None, :], ypad)
        x2 = h
    return h.reshape(batch, seq, d)
```

```python
import functools

import jax
import jax.numpy as jnp
from jax import lax
from jax.experimental import pallas as pl
from jax.experimental.pallas import tpu as pltpu

D_MODEL = 1024
CONV_WIDTH = 1024
CONV_K = 3
N_HEADS = 8
HEAD_DIM = 64
ATTN_WIDTH = N_HEADS * HEAD_DIM
MOBA_BLOCK = 256
MOBA_TOPK = 3
ROPE_THETA = 500000.0
ROPE_DIM = HEAD_DIM // 4
N_EXPERTS = 256
TOP_K = 8
N_GROUPS = 8
TOPK_GROUPS = 4
EXPERT_DIM = 256
SHARED_DIM = 256
ROUTED_SCALE = 2.5
DEPTH = 1
DN_ALPHA = (2 * DEPTH) ** 0.25
LN_EPS = 1e-5

NEG_BIG = -1e30
QK_SCALE = HEAD_DIM ** -0.5
assert QK_SCALE == 2.0 ** -3

VMEM_LIMIT_BYTES = 56 * 1024 * 1024

PROJ_ROWS = 512
ROUTE_COLS = 256
EXPERT_ROWS = 256
DISPATCH_ROWS = 256
COMBINE_ROWS = 256

F32 = jnp.float32
BF16 = jnp.bfloat16
U32 = jnp.uint32
I32 = jnp.int32


def _sigmoid(v):
    return 1.0 / (1.0 + jnp.exp(-v))


def _dot(a, b):
    return jnp.dot(a, b, preferred_element_type=F32)


def _dot_nt(a, b):
    return lax.dot_general(a, b, (((1,), (1,)), ((), ())), preferred_element_type=F32)


def _dot_tn(a, b):
    return lax.dot_general(a, b, (((0,), (0,)), ((), ())), preferred_element_type=F32)


ROW_TILE = 8
LANES = 128


def _store_row_tiles(ref, val):
    m = val.shape[0]
    for s in range(ROW_TILE):
        ref[pl.ds(s, m, stride=ROW_TILE), :] = val[:, s * LANES:(s + 1) * LANES]


def _load_row_tiles(ref, m, dtype=None):
    parts = [ref[pl.ds(s, m, stride=ROW_TILE), :] for s in range(ROW_TILE)]
    if dtype is not None:
        parts = [p.astype(dtype) for p in parts]
    return jnp.concatenate(parts, axis=1)


def _layer_norm(r, g, b):
    mu = jnp.mean(r, axis=-1, keepdims=True)
    c = r - mu
    var = jnp.mean(c * c, axis=-1, keepdims=True)
    return c * lax.rsqrt(var + LN_EPS) * g + b


def _proj_kernel(x_ref, wc_ref, wqkv_ref, wg_ref, bg_ref, wconv_ref, wco_ref,
                 cos_ref, sinlo_ref, sinhi_ref,
                 a_ref, gattn_ref, q_ref, k_ref, vt_ref, kmean_ref,
                 halo_ref, *, tiles_per_seq):
    tm = x_ref.shape[0]
    i = pl.program_id(0)
    xb = x_ref[...].astype(BF16)

    cb = _dot(xb, wc_ref[:, 0:CONV_WIDTH])
    cc = _dot(xb, wc_ref[:, CONV_WIDTH:2 * CONV_WIDTH])
    cv = _dot(xb, wc_ref[:, 2 * CONV_WIDTH:3 * CONV_WIDTH])
    u = cc * cv
    @pl.when((i % tiles_per_seq) == 0)
    def _():
        halo_ref[...] = jnp.zeros(halo_ref.shape, F32)

    prev = halo_ref[...]
    row = lax.broadcasted_iota(I32, u.shape, 0)
    u1 = jnp.where(row == 0, prev[7:8, :], pltpu.roll(u, 1, 0))
    u2 = jnp.where(row == 0, prev[6:7, :], jnp.where(row == 1, prev[7:8, :], pltpu.roll(u, 2, 0)))
    halo_ref[...] = u[tm - 8:tm, :]
    w0 = wconv_ref[0:1, :]
    w1 = wconv_ref[1:2, :]
    w2 = wconv_ref[2:3, :]
    yc = cb * (w0 * u2 + w1 * u1 + w2 * u)
    y_conv = _dot(yc.astype(BF16), wco_ref[...])

    g = _dot(xb, wg_ref[...]) + bg_ref[...]
    gates = _sigmoid(g)
    a_ref[...] = (gates[:, :D_MODEL] * y_conv).astype(a_ref.dtype)
    gattn_ref[...] = gates[:, D_MODEL:].astype(gattn_ref.dtype)

    qkv = _dot(xb, wqkv_ref[...])
    cos = cos_ref[...]
    sinlo = sinlo_ref[...]
    sinhi = sinhi_ref[...]

    def rope(t):
        return (t * cos + pltpu.roll(t, ATTN_WIDTH - ROPE_DIM // 2, 1) * sinlo
                + pltpu.roll(t, ROPE_DIM // 2, 1) * sinhi)

    q = rope(qkv[:, 0:ATTN_WIDTH])
    k = rope(qkv[:, ATTN_WIDTH:2 * ATTN_WIDTH])
    q_ref[...] = (q * QK_SCALE).astype(q_ref.dtype)
    k_ref[...] = k.astype(k_ref.dtype)
    v = qkv[:, 2 * ATTN_WIDTH:]
    for blk in range(tm // MOBA_BLOCK):
        rows = slice(blk * MOBA_BLOCK, (blk + 1) * MOBA_BLOCK)
        kmean_ref[blk] = jnp.mean(k[rows, :], axis=0, keepdims=True)
        vt_ref[blk] = v[rows, :].T.astype(vt_ref.dtype)


def _proj(x2, wc, wqkv, wg, bg, wconv, wco, cos, sinlo, sinhi, seq):
    t = x2.shape[0]
    tm = PROJ_ROWS
    tiles_per_seq = seq // tm
    const = lambda i: (0, 0)
    rows = lambda i: (i, 0)
    pos = lambda i: (i % tiles_per_seq, 0)
    return pl.pallas_call(
        functools.partial(_proj_kernel, tiles_per_seq=tiles_per_seq),
        name="proj",
        grid=(t // tm,),
        in_specs=[
            pl.BlockSpec((tm, D_MODEL), rows),
            pl.BlockSpec(wc.shape, const),
            pl.BlockSpec(wqkv.shape, const),
            pl.BlockSpec(wg.shape, const),
            pl.BlockSpec(bg.shape, const),
            pl.BlockSpec(wconv.shape, const),
            pl.BlockSpec(wco.shape, const),
            pl.BlockSpec((tm, ATTN_WIDTH), pos),
            pl.BlockSpec((tm, ATTN_WIDTH), pos),
            pl.BlockSpec((tm, ATTN_WIDTH), pos),
        ],
        out_specs=[
            pl.BlockSpec((tm, D_MODEL), rows),
            pl.BlockSpec((tm, D_MODEL), rows),
            pl.BlockSpec((tm, ATTN_WIDTH), rows),
            pl.BlockSpec((tm, ATTN_WIDTH), rows),
            pl.BlockSpec((tm // MOBA_BLOCK, ATTN_WIDTH, MOBA_BLOCK), lambda i: (i, 0, 0)),
            pl.BlockSpec((tm // MOBA_BLOCK, 1, ATTN_WIDTH), lambda i: (i, 0, 0)),
        ],
        out_shape=[
            jax.ShapeDtypeStruct((t, D_MODEL), BF16),
            jax.ShapeDtypeStruct((t, D_MODEL), BF16),
            jax.ShapeDtypeStruct((t, ATTN_WIDTH), BF16),
            jax.ShapeDtypeStruct((t, ATTN_WIDTH), BF16),
            jax.ShapeDtypeStruct((t // MOBA_BLOCK, ATTN_WIDTH, MOBA_BLOCK), BF16),
            jax.ShapeDtypeStruct((t // MOBA_BLOCK, 1, ATTN_WIDTH), F32),
        ],
        scratch_shapes=[pltpu.VMEM((8, CONV_WIDTH), F32)],
        compiler_params=pltpu.CompilerParams(
            dimension_semantics=("arbitrary",), vmem_limit_bytes=VMEM_LIMIT_BYTES),
    )(x2, wc, wqkv, wg, bg, wconv, wco, cos, sinlo, sinhi)


def _attn_kernel(q_ref, k_ref, vt_ref, kmean_ref, o_ref, bias_ref, m_ref, l_ref, alpha_ref, p_ref, acc_ref,
                 *, n_blocks):
    j = pl.program_id(1)
    blk_rows = MOBA_BLOCK
    key_i = lax.broadcasted_iota(I32, (blk_rows, blk_rows), 0)
    qry_i = lax.broadcasted_iota(I32, (blk_rows, blk_rows), 1)
    causal = key_i <= qry_i
    blk_i = lax.broadcasted_iota(I32, (n_blocks, blk_rows), 0)
    past = blk_i < j

    for h in range(N_HEADS):
        cols = slice(h * HEAD_DIM, (h + 1) * HEAD_DIM)
        qh = q_ref[:, cols]
        km = kmean_ref[:, 0, cols]
        gate = _dot_nt(km, qh.astype(F32))
        gate = jnp.where(past, gate, -jnp.inf)
        rank = jnp.zeros(gate.shape, F32)
        for m in range(n_blocks):
            gm = gate[m:m + 1, :]
            ahead = jnp.where(gm > gate, 1.0, jnp.where(gm == gate, jnp.where(blk_i > m, 1.0, 0.0), 0.0))
            rank = rank + ahead
        bias_ref[h] = jnp.where(past, jnp.where(rank < MOBA_TOPK, 0.0, NEG_BIG), NEG_BIG)

    def scores(n):
        r = pl.multiple_of(n * blk_rows, blk_rows)
        return [_dot_nt(k_ref[pl.ds(r, blk_rows), h * HEAD_DIM:(h + 1) * HEAD_DIM],
                        q_ref[:, h * HEAD_DIM:(h + 1) * HEAD_DIM]) for h in range(N_HEADS)]

    for h, s_raw in enumerate(scores(j)):
        cols = slice(h * HEAD_DIM, (h + 1) * HEAD_DIM)
        s = jnp.where(causal, s_raw, NEG_BIG)
        m0 = jnp.max(s, axis=0, keepdims=True)
        p = jnp.exp(s - m0)
        m_ref[h:h + 1, :] = m0
        l_ref[h:h + 1, :] = jnp.sum(p, axis=0, keepdims=True)
        p_ref[h] = p.astype(p_ref.dtype)
    for h in range(N_HEADS):
        cols = slice(h * HEAD_DIM, (h + 1) * HEAD_DIM)
        acc_ref[cols, :] = _dot(vt_ref[j, cols, :], p_ref[h])

    def body(n, carry):
        for h, s_raw in enumerate(scores(n)):
            sn = s_raw + bias_ref[h, pl.ds(n, 1), :]
            m_run = m_ref[h:h + 1, :]
            m_new = jnp.maximum(m_run, jnp.max(sn, axis=0, keepdims=True))
            alpha = jnp.exp(m_run - m_new)
            pn = jnp.exp(sn - m_new)
            m_ref[h:h + 1, :] = m_new
            l_ref[h:h + 1, :] = alpha * l_ref[h:h + 1, :] + jnp.sum(pn, axis=0, keepdims=True)
            alpha_ref[h:h + 1, :] = alpha
            p_ref[h] = pn.astype(p_ref.dtype)
        pvs = [_dot(vt_ref[n, h * HEAD_DIM:(h + 1) * HEAD_DIM, :], p_ref[h]) for h in range(N_HEADS)]
        for h in range(N_HEADS):
            cols = slice(h * HEAD_DIM, (h + 1) * HEAD_DIM)
            acc_ref[cols, :] = acc_ref[cols, :] * alpha_ref[h:h + 1, :] + pvs[h]
        return carry

    lax.fori_loop(0, j, body, 0)
    for h in range(N_HEADS):
        cols = slice(h * HEAD_DIM, (h + 1) * HEAD_DIM)
        acc_ref[cols, :] = acc_ref[cols, :] / l_ref[h:h + 1, :]
    o_ref[...] = acc_ref[...].T.astype(o_ref.dtype)


def _attention(q, k, vt, kmean, batch, seq):
    n_blocks = seq // MOBA_BLOCK
    return pl.pallas_call(
        functools.partial(_attn_kernel, n_blocks=n_blocks),
        name="attention",
        grid=(batch, n_blocks),
        in_specs=[
            pl.BlockSpec((MOBA_BLOCK, ATTN_WIDTH), lambda b, j: (b * n_blocks + j, 0)),
            pl.BlockSpec((seq, ATTN_WIDTH), lambda b, j: (b, 0)),
            pl.BlockSpec((n_blocks, ATTN_WIDTH, MOBA_BLOCK), lambda b, j: (b, 0, 0)),
            pl.BlockSpec((n_blocks, 1, ATTN_WIDTH), lambda b, j: (b, 0, 0)),
        ],
        out_specs=pl.BlockSpec((MOBA_BLOCK, ATTN_WIDTH), lambda b, j: (b * n_blocks + j, 0)),
        out_shape=jax.ShapeDtypeStruct(q.shape, BF16),
        scratch_shapes=[
            pltpu.VMEM((N_HEADS, n_blocks, MOBA_BLOCK), F32),
            pltpu.VMEM((N_HEADS, MOBA_BLOCK), F32),
            pltpu.VMEM((N_HEADS, MOBA_BLOCK), F32),
            pltpu.VMEM((N_HEADS, MOBA_BLOCK), F32),
            pltpu.VMEM((N_HEADS, MOBA_BLOCK, MOBA_BLOCK), BF16),
            pltpu.VMEM((ATTN_WIDTH, MOBA_BLOCK), F32),
        ],
        compiler_params=pltpu.CompilerParams(
            dimension_semantics=("arbitrary", "arbitrary"), vmem_limit_bytes=VMEM_LIMIT_BYTES),
    )(q, k, vt, kmean)


def _post_kernel(o_ref, a_ref, gattn_ref, x_ref, wao_ref, wmo_ref, g1_ref, b1_ref,
                 ws1_ref, ws3_ref, ws2_ref, wrt_ref,
                 base_ref, ht_ref, logit_ref):
    y_attn = _dot(o_ref[...], wao_ref[...])
    merged = a_ref[...].astype(F32) + gattn_ref[...].astype(F32) * y_attn
    mix = _dot(merged.astype(BF16), wmo_ref[...])
    h = _layer_norm(DN_ALPHA * x_ref[...] + mix, g1_ref[...], b1_ref[...])
    hb = h.astype(BF16)
    s1 = _dot(hb, ws1_ref[...])
    s3 = _dot(hb, ws3_ref[...])
    shared = _dot((s1 * _sigmoid(s1) * s3).astype(BF16), ws2_ref[...])
    base_ref[...] = DN_ALPHA * h + shared
    _store_row_tiles(ht_ref, h)
    logit_ref[...] = _dot_nt(wrt_ref[...], h)


def _post(o, a, gattn, x2, wao, wmo, g1, b1, ws1, ws3, ws2, wrt):
    t = x2.shape[0]
    tm = PROJ_ROWS
    const = lambda i: (0, 0)
    rows = lambda i: (i, 0)
    return pl.pallas_call(
        _post_kernel,
        name="post",
        grid=(t // tm,),
        in_specs=[
            pl.BlockSpec((tm, ATTN_WIDTH), rows),
            pl.BlockSpec((tm, D_MODEL), rows),
            pl.BlockSpec((tm, D_MODEL), rows),
            pl.BlockSpec((tm, D_MODEL), rows),
            pl.BlockSpec(wao.shape, const),
            pl.BlockSpec(wmo.shape, const),
            pl.BlockSpec(g1.shape, const),
            pl.BlockSpec(b1.shape, const),
            pl.BlockSpec(ws1.shape, const),
            pl.BlockSpec(ws3.shape, const),
            pl.BlockSpec(ws2.shape, const),
            pl.BlockSpec(wrt.shape, const),
        ],
        out_specs=[
            pl.BlockSpec((tm, D_MODEL), rows),
            pl.BlockSpec((tm * ROW_TILE, LANES), rows),
            pl.BlockSpec((N_EXPERTS, tm), lambda i: (0, i)),
        ],
        out_shape=[
            jax.ShapeDtypeStruct((t, D_MODEL), F32),
            jax.ShapeDtypeStruct((t * ROW_TILE, LANES), F32),
            jax.ShapeDtypeStruct((N_EXPERTS, t), F32),
        ],
        compiler_params=pltpu.CompilerParams(
            dimension_semantics=("arbitrary",), vmem_limit_bytes=VMEM_LIMIT_BYTES),
    )(o, a, gattn, x2, wao, wmo, g1, b1, ws1, ws3, ws2, wrt)


def _route_kernel(logit_ref, bias_ref, tri_ref, eid_ref, pos_ref, wt_ref, cnt_ref, run_ref):
    i = pl.program_id(0)
    tm = logit_ref.shape[1]
    per_group = N_EXPERTS // N_GROUPS

    @pl.when(i == 0)
    def _():
        run_ref[...] = jnp.zeros(run_ref.shape, F32)

    scores = _sigmoid(logit_ref[...])
    biased = scores + bias_ref[...]
    b3 = biased.reshape(N_GROUPS, per_group, tm)
    i3 = lax.broadcasted_iota(I32, b3.shape, 1)
    m1 = jnp.max(b3, axis=1)
    first = jnp.min(jnp.where(b3 == m1[:, None, :], i3, per_group), axis=1)
    m2 = jnp.max(jnp.where(i3 == first[:, None, :], -jnp.inf, b3), axis=1)
    gscore = m1 + m2
    g_i = lax.broadcasted_iota(I32, gscore.shape, 0)
    grank = jnp.zeros(gscore.shape, F32)
    for m in range(N_GROUPS):
        gm = gscore[m:m + 1, :]
        ahead = jnp.where(gm > gscore, 1.0, jnp.where(gm == gscore, jnp.where(g_i > m, 1.0, 0.0), 0.0))
        grank = grank + ahead
    gsel = grank < TOPK_GROUPS
    work = jnp.where(gsel[:, None, :], b3, -jnp.inf).reshape(N_EXPERTS, tm)

    e_i = lax.broadcasted_iota(I32, (N_EXPERTS, tm), 0)
    sel = jnp.zeros((N_EXPERTS, tm), F32)
    idxs = []
    scs = []
    for _k in range(TOP_K):
        mx = jnp.max(work, axis=0, keepdims=True)
        idx = jnp.min(jnp.where(work == mx, e_i, N_EXPERTS), axis=0, keepdims=True)
        hit = e_i == idx
        scs.append(jnp.sum(jnp.where(hit, scores, 0.0), axis=0, keepdims=True))
        sel = sel + jnp.where(hit, 1.0, 0.0)
        work = jnp.where(hit, -jnp.inf, work)
        idxs.append(idx)

    rank = _dot(sel.astype(BF16), tri_ref[...]) + run_ref[...]
    run_ref[...] = run_ref[...] + jnp.sum(sel, axis=1, keepdims=True)
    cnt_ref[...] = run_ref[...]

    wsum = scs[0]
    for kk in range(1, TOP_K):
        wsum = wsum + scs[kk]
    for kk in range(TOP_K):
        pos = jnp.sum(jnp.where(e_i == idxs[kk], rank, 0.0), axis=0, keepdims=True)
        eid_ref[kk:kk + 1, :] = idxs[kk]
        pos_ref[kk:kk + 1, :] = pos.astype(I32)
        wt_ref[kk:kk + 1, :] = scs[kk] / wsum * ROUTED_SCALE


def _route(logits_t, bias_col, tri):
    t = logits_t.shape[1]
    tm = ROUTE_COLS
    cols = lambda i: (0, i)
    const = lambda i: (0, 0)
    return pl.pallas_call(
        _route_kernel,
        name="route",
        grid=(t // tm,),
        in_specs=[
            pl.BlockSpec((N_EXPERTS, tm), cols),
            pl.BlockSpec((N_EXPERTS, 1), const),
            pl.BlockSpec((tm, tm), const),
        ],
        out_specs=[
            pl.BlockSpec((TOP_K, tm), cols),
            pl.BlockSpec((TOP_K, tm), cols),
            pl.BlockSpec((TOP_K, tm), cols),
            pl.BlockSpec((N_EXPERTS, 1), const),
        ],
        out_shape=[
            jax.ShapeDtypeStruct((TOP_K, t), I32),
            jax.ShapeDtypeStruct((TOP_K, t), I32),
            jax.ShapeDtypeStruct((TOP_K, t), F32),
            jax.ShapeDtypeStruct((N_EXPERTS, 1), F32),
        ],
        scratch_shapes=[pltpu.VMEM((N_EXPERTS, 1), F32)],
        compiler_params=pltpu.CompilerParams(
            dimension_semantics=("arbitrary",), vmem_limit_bytes=VMEM_LIMIT_BYTES),
    )(logits_t, bias_col, tri)


def _slots_kernel(eid_ref, pos_ref, pstart_ref, dest_ref):
    tm = eid_ref.shape[1]
    e_i = lax.broadcasted_iota(I32, (N_EXPERTS, tm), 0)
    pstart = pstart_ref[...]
    for kk in range(TOP_K):
        start = jnp.sum(jnp.where(e_i == eid_ref[kk:kk + 1, :], pstart, 0.0), axis=0, keepdims=True)
        dest_ref[kk:kk + 1, :] = start.astype(I32) + pos_ref[kk:kk + 1, :]


def _slots(eid, pos, pstart_col):
    t = eid.shape[1]
    tm = ROUTE_COLS
    cols = lambda i: (0, i)
    return pl.pallas_call(
        _slots_kernel,
        name="slots",
        grid=(t // tm,),
        in_specs=[
            pl.BlockSpec((TOP_K, tm), cols),
            pl.BlockSpec((TOP_K, tm), cols),
            pl.BlockSpec((N_EXPERTS, 1), lambda i: (0, 0)),
        ],
        out_specs=pl.BlockSpec((TOP_K, tm), cols),
        out_shape=jax.ShapeDtypeStruct((TOP_K, t), I32),
        compiler_params=pltpu.CompilerParams(dimension_semantics=("arbitrary",)),
    )(eid, pos, pstart_col)


def _row_tile_at(ref, row):
    return ref.at[pl.ds(pl.multiple_of(row * ROW_TILE, ROW_TILE), ROW_TILE)]


def _dispatch_kernel(dest_ref, pad_lo_ref, pad_hi_ref, nvalid_ref, ht_ref, xpad_ref,
                     stage_ref, zero_ref, sem, pad_sem, tail_sem, *, experts_per_step):
    i = pl.program_id(0)
    n = pl.num_programs(0)
    tm = DISPATCH_ROWS
    slot = i % 2

    def pad_range(step, q):
        e = step * experts_per_step + q
        ec = jnp.minimum(e, N_EXPERTS - 1)
        lo = pad_lo_ref[ec]
        return lo, jnp.where(e < N_EXPERTS, pad_hi_ref[ec], lo)

    def start_pad(step):
        for q in range(experts_per_step):
            lo, hi = pad_range(step, q)

            def start_zero(s, c):
                pltpu.make_async_copy(
                    _row_tile_at(zero_ref, s - lo), _row_tile_at(xpad_ref, s), pad_sem).start()
                return c

            lax.fori_loop(lo, hi, start_zero, 0)

    def wait_pad(step):
        for q in range(experts_per_step):
            lo, hi = pad_range(step, q)
            count = hi - lo
            p = EXPERT_ROWS // 2
            while p >= 1:
                @pl.when((count & p) != 0)
                def _(p=p):
                    pltpu.make_async_copy(
                        zero_ref.at[pl.ds(0, p * ROW_TILE)], xpad_ref.at[pl.ds(0, p * ROW_TILE)], pad_sem).wait()
                p //= 2

    def tail_copy(tile):
        rows = EXPERT_ROWS * ROW_TILE
        return pltpu.make_async_copy(
            zero_ref, xpad_ref.at[pl.ds(pl.multiple_of(tile * rows, rows), rows)], tail_sem)

    def start_tail(tile, c):
        tail_copy(tile).start()
        return c

    def wait_tail(tile, c):
        tail_copy(tile).wait()
        return c

    n_tiles = xpad_ref.shape[0] // (EXPERT_ROWS * ROW_TILE)

    @pl.when(i == 0)
    def _():
        zero_ref[...] = jnp.zeros(zero_ref.shape, zero_ref.dtype)
        lax.fori_loop(nvalid_ref[0], n_tiles, start_tail, 0)

    start_pad(i)

    stage_ref[slot] = ht_ref[...]

    def issue(r, carry):
        src = _row_tile_at(stage_ref.at[slot], r)
        for kk in range(TOP_K):
            pltpu.make_async_copy(
                src, _row_tile_at(xpad_ref, dest_ref[r * TOP_K + kk]), sem.at[slot]).start(priority=kk % 2)
        return carry

    lax.fori_loop(0, tm, issue, 0)

    def drain(s):
        for _kk in range(TOP_K):
            pltpu.make_async_copy(
                stage_ref.at[s], xpad_ref.at[pl.ds(0, tm * ROW_TILE)], sem.at[s]).wait()

    @pl.when(i > 0)
    def _():
        drain(1 - slot)
        wait_pad(i - 1)

    @pl.when(i == n - 1)
    def _():
        drain(slot)
        wait_pad(i)
        lax.fori_loop(nvalid_ref[0], n_tiles, wait_tail, 0)


def _dispatch(dest_flat, pad_lo, pad_hi, n_valid, ht, cap):
    t = ht.shape[0] // ROW_TILE
    tm = DISPATCH_ROWS
    experts_per_step = -(-N_EXPERTS // (t // tm))
    return pl.pallas_call(
        functools.partial(_dispatch_kernel, experts_per_step=experts_per_step),
        name="dispatch",
        grid=(t // tm,),
        in_specs=[
            pl.BlockSpec((tm * TOP_K,), lambda i: (i,), memory_space=pltpu.SMEM),
            pl.BlockSpec(memory_space=pltpu.SMEM),
            pl.BlockSpec(memory_space=pltpu.SMEM),
            pl.BlockSpec(memory_space=pltpu.SMEM),
            pl.BlockSpec((tm * ROW_TILE, LANES), lambda i: (i, 0)),
        ],
        out_specs=pl.BlockSpec(memory_space=pl.ANY),
        out_shape=jax.ShapeDtypeStruct((cap * ROW_TILE, LANES), F32),
        scratch_shapes=[
            pltpu.VMEM((2, tm * ROW_TILE, LANES), F32),
            pltpu.VMEM((EXPERT_ROWS * ROW_TILE, LANES), F32),
            pltpu.SemaphoreType.DMA((2,)),
            pltpu.SemaphoreType.DMA(()),
            pltpu.SemaphoreType.DMA(()),
        ],
        compiler_params=pltpu.CompilerParams(
            dimension_semantics=("arbitrary",), has_side_effects=True),
    )(dest_flat, pad_lo, pad_hi, n_valid, ht)


def _expert_kernel(texp_ref, nvalid_ref, first_ref, next_ref, par_ref,
                   x_ref, w1_hbm, w3_hbm, w2_hbm, y_ref,
                   w1f_ref, w3f_ref, w2f_ref, w1b_ref, w3b_ref, w2b_ref, sem):
    i = pl.program_id(0)
    tm = EXPERT_ROWS
    valid = i < nvalid_ref[0]

    def weight_copies(e, s):
        return (pltpu.make_async_copy(w1_hbm.at[e], w1f_ref.at[s], sem.at[s]),
                pltpu.make_async_copy(w3_hbm.at[e], w3f_ref.at[s], sem.at[s]),
                pltpu.make_async_copy(w2_hbm.at[e], w2f_ref.at[s], sem.at[s]))

    @pl.when(jnp.logical_and(valid, first_ref[i] == 1))
    def _():
        s = par_ref[i]

        @pl.when(i == 0)
        def _():
            for c in weight_copies(texp_ref[0], 0):
                c.start()

        for c in weight_copies(texp_ref[i], s):
            c.wait()
        nxt = next_ref[i]

        @pl.when(nxt >= 0)
        def _():
            for c in weight_copies(nxt, 1 - s):
                c.start()

        w1b_ref[...] = w1f_ref[s].astype(BF16)
        w3b_ref[...] = w3f_ref[s].astype(BF16)
        w2b_ref[...] = w2f_ref[s].astype(BF16)

    @pl.when(valid)
    def _():
        xb = _load_row_tiles(x_ref, tm, BF16)
        g = _dot(xb, w1b_ref[...])
        u = _dot(xb, w3b_ref[...])
        hid = (g * _sigmoid(g) * u).astype(BF16)
        _store_row_tiles(y_ref, _dot(hid, w2b_ref[...]))

    @pl.when(jnp.logical_not(valid))
    def _():
        y_ref[...] = jnp.zeros(y_ref.shape, y_ref.dtype)


def _experts(tile_expert, n_valid, tile_first, tile_next, tile_par, xpad, w1, w3, w2):
    tm = EXPERT_ROWS
    n_tiles = xpad.shape[0] // (tm * ROW_TILE)

    def x_map(i, te, nv, fi, nx, pa):
        return (jnp.minimum(i, jnp.maximum(nv[0] - 1, 0)), 0)

    grid_spec = pltpu.PrefetchScalarGridSpec(
        num_scalar_prefetch=5,
        grid=(n_tiles,),
        in_specs=[
            pl.BlockSpec((tm * ROW_TILE, LANES), x_map),
            pl.BlockSpec(memory_space=pl.ANY),
            pl.BlockSpec(memory_space=pl.ANY),
            pl.BlockSpec(memory_space=pl.ANY),
        ],
        out_specs=pl.BlockSpec((tm * ROW_TILE, LANES), lambda i, te, nv, fi, nx, pa: (i, 0)),
        scratch_shapes=[
            pltpu.VMEM((2, D_MODEL, EXPERT_DIM), F32),
            pltpu.VMEM((2, D_MODEL, EXPERT_DIM), F32),
            pltpu.VMEM((2, EXPERT_DIM, D_MODEL), F32),
            pltpu.VMEM((D_MODEL, EXPERT_DIM), BF16),
            pltpu.VMEM((D_MODEL, EXPERT_DIM), BF16),
            pltpu.VMEM((EXPERT_DIM, D_MODEL), BF16),
            pltpu.SemaphoreType.DMA((2,)),
        ],
    )
    return pl.pallas_call(
        _expert_kernel,
        name="experts",
        grid_spec=grid_spec,
        out_shape=jax.ShapeDtypeStruct(xpad.shape, F32),
        compiler_params=pltpu.CompilerParams(
            dimension_semantics=("arbitrary",), vmem_limit_bytes=VMEM_LIMIT_BYTES),
    )(tile_expert, n_valid, tile_first, tile_next, tile_par, xpad, w1, w3, w2)


def _combine_kernel(dest_ref, dest_next_ref, wt_ref, base_ref, g2_ref, b2_ref, ypad_ref, out_ref, buf_ref, sem):
    i = pl.program_id(0)
    n = pl.num_programs(0)
    tm = COMBINE_ROWS
    slot = i % 2

    def issue_from(d_ref, s):
        def issue(r, carry):
            for kk in range(TOP_K):
                pltpu.make_async_copy(
                    _row_tile_at(ypad_ref, d_ref[r * TOP_K + kk]), _row_tile_at(buf_ref.at[s, kk], r),
                    sem.at[s]).start(priority=kk % 2)
            return carry
        lax.fori_loop(0, tm, issue, 0)

    @pl.when(i == 0)
    def _():
        issue_from(dest_ref, 0)

    @pl.when(i + 1 < n)
    def _():
        issue_from(dest_next_ref, 1 - slot)

    for kk in range(TOP_K):
        pltpu.make_async_copy(
            ypad_ref.at[pl.ds(0, tm * ROW_TILE)], buf_ref.at[slot, kk], sem.at[slot]).wait()

    wt = wt_ref[...]
    wks = [wt[:, kk:kk + 1] for kk in range(TOP_K)]
    parts = []
    for s in range(ROW_TILE):
        acc = base_ref[:, s * LANES:(s + 1) * LANES]
        for kk in range(TOP_K):
            acc = acc + wks[kk] * buf_ref[slot, kk, pl.ds(s, tm, stride=ROW_TILE), :]
        parts.append(acc)
    r = jnp.concatenate(parts, axis=1)
    out_ref[...] = _layer_norm(r, g2_ref[...], b2_ref[...])


def _combine(dest_flat, wt_tok, base, g2, b2, ypad):
    t = base.shape[0]
    tm = COMBINE_ROWS
    n = t // tm
    rows = lambda i: (i, 0)
    const = lambda i: (0, 0)
    return pl.pallas_call(
        _combine_kernel,
        name="combine",
        grid=(n,),
        in_specs=[
            pl.BlockSpec((tm * TOP_K,), lambda i: (i,), memory_space=pltpu.SMEM),
            pl.BlockSpec((tm * TOP_K,), lambda i: (jnp.minimum(i + 1, n - 1),), memory_space=pltpu.SMEM),
            pl.BlockSpec((tm, TOP_K), rows),
            pl.BlockSpec((tm, D_MODEL), rows),
            pl.BlockSpec(g2.shape, const),
            pl.BlockSpec(b2.shape, const),
            pl.BlockSpec(memory_space=pl.ANY),
        ],
        out_specs=pl.BlockSpec((tm, D_MODEL), rows),
        out_shape=jax.ShapeDtypeStruct((t, D_MODEL), F32),
        scratch_shapes=[
            pltpu.VMEM((2, TOP_K, tm * ROW_TILE, LANES), F32),
            pltpu.SemaphoreType.DMA((2,)),
        ],
        compiler_params=pltpu.CompilerParams(
            dimension_semantics=("arbitrary",), vmem_limit_bytes=VMEM_LIMIT_BYTES),
    )(dest_flat, dest_flat, wt_tok, base, g2, b2, ypad)


def _rope_tables(seq):
    half = ROPE_DIM // 2
    inv = jnp.power(ROPE_THETA, -jnp.arange(half, dtype=F32) * 2.0 / ROPE_DIM)
    ang = jnp.arange(seq, dtype=F32)[:, None] * inv[None, :]
    cos = jnp.cos(ang)
    sin = jnp.sin(ang)
    ones = jnp.ones((seq, HEAD_DIM - ROPE_DIM), F32)
    zeros_rest = jnp.zeros((seq, HEAD_DIM - ROPE_DIM), F32)
    zeros_half = jnp.zeros((seq, half), F32)
    cos_h = jnp.concatenate([cos, cos, ones], axis=1)
    sinlo_h = jnp.concatenate([-sin, zeros_half, zeros_rest], axis=1)
    sinhi_h = jnp.concatenate([zeros_half, sin, zeros_rest], axis=1)
    tile = lambda m: jnp.tile(m, (1, N_HEADS))
    return tile(cos_h), tile(sinlo_h), tile(sinhi_h)


def kernel(x, w_in, b_gate, w_conv, w_conv_out, w_attn_out, w_mix_out, ln1_g, ln1_b, w_router, router_bias,
           w1, w3, w2, ws1, ws3, ws2, ln2_g, ln2_b):
    batch, seq, d = x.shape
    t = batch * seq
    assert d == D_MODEL and seq % PROJ_ROWS == 0 and seq % MOBA_BLOCK == 0
    x2 = x.reshape(t, d)
    h = None
    for layer in range(DEPTH):
        c3 = 3 * CONV_WIDTH
        a3 = 3 * ATTN_WIDTH
        w_in_b = w_in[layer].astype(BF16)
        wc = w_in_b[:, :c3]
        wqkv = w_in_b[:, c3:c3 + a3]
        wg = w_in_b[:, c3 + a3:]
        cos, sinlo, sinhi = _rope_tables(seq)
        a, gattn, q, k, v, kmean = _proj(
            x2, wc, wqkv, wg, b_gate[layer][None, :], w_conv[layer], w_conv_out[layer].astype(BF16),
            cos, sinlo, sinhi, seq)
        o = _attention(q, k, v, kmean, batch, seq)
        base, ht, logits_t = _post(
            o, a, gattn, x2, w_attn_out[layer].astype(BF16), w_mix_out[layer].astype(BF16),
            ln1_g[layer][None, :], ln1_b[layer][None, :],
            ws1[layer].astype(BF16), ws3[layer].astype(BF16), ws2[layer].astype(BF16),
            w_router[layer].T)

        tri = (jnp.arange(ROUTE_COLS)[:, None] < jnp.arange(ROUTE_COLS)[None, :]).astype(BF16)
        eid, pos, wts, counts = _route(logits_t, router_bias[layer][:, None], tri)

        cnt = counts[:, 0].astype(I32)
        padded = (cnt + EXPERT_ROWS - 1) // EXPERT_ROWS * EXPERT_ROWS
        pend = jnp.cumsum(padded)
        pstart = pend - padded
        cap = t * TOP_K + N_EXPERTS * EXPERT_ROWS
        n_tiles = cap // EXPERT_ROWS
        dest = _slots(eid, pos, pstart.astype(F32)[:, None])
        dest_flat = dest.T.reshape(t * TOP_K)
        tile_start = jnp.arange(n_tiles, dtype=I32) * EXPERT_ROWS
        tile_expert = jnp.minimum(
            jnp.sum((pend[None, :] <= tile_start[:, None]).astype(I32), axis=1), N_EXPERTS - 1)
        n_valid = (pend[-1:] // EXPERT_ROWS).astype(I32)
        tile_i = jnp.arange(n_tiles, dtype=I32)
        prev_expert = jnp.concatenate([jnp.full((1,), -1, I32), tile_expert[:-1]])
        tile_first = ((tile_i < n_valid[0]) & (tile_expert != prev_expert)).astype(I32)
        first_pos = jnp.where(tile_first == 1, tile_i, n_tiles)
        later_first = lax.cummin(jnp.concatenate([first_pos[1:], jnp.full((1,), n_tiles, I32)]), reverse=True)
        tile_next = jnp.where(later_first < n_tiles, tile_expert[jnp.minimum(later_first, n_tiles - 1)], -1)
        tile_par = (jnp.cumsum(tile_first) - 1) % 2

        xpad = _dispatch(dest_flat, pstart + cnt, pend, n_valid, ht, cap)
        ypad = _experts(tile_expert, n_valid, tile_first, tile_next.astype(I32), tile_par.astype(I32),
                        xpad, w1[layer], w3[layer], w2[layer])
        h = _combine(dest_flat, wts.T, base, ln2_g[layer][None, :], ln2_b[layer][None, :], ypad)
        x2 = h
    return h.reshape(batch, seq, d)
```

```python
import functools

import jax
import jax.numpy as jnp
from jax import lax
from jax.experimental import pallas as pl
from jax.experimental.pallas import tpu as pltpu

D_MODEL = 1024
CONV_WIDTH = 1024
CONV_K = 3
N_HEADS = 8
HEAD_DIM = 64
ATTN_WIDTH = N_HEADS * HEAD_DIM
MOBA_BLOCK = 256
MOBA_TOPK = 3
ROPE_THETA = 500000.0
ROPE_DIM = HEAD_DIM // 4
N_EXPERTS = 256
TOP_K = 8
N_GROUPS = 8
TOPK_GROUPS = 4
EXPERT_DIM = 256
SHARED_DIM = 256
ROUTED_SCALE = 2.5
DEPTH = 1
DN_ALPHA = (2 * DEPTH) ** 0.25
LN_EPS = 1e-5

NEG_BIG = -1e30
QK_SCALE = HEAD_DIM ** -0.5
assert QK_SCALE == 2.0 ** -3

VMEM_LIMIT_BYTES = 56 * 1024 * 1024

PROJ_ROWS = 512
ROUTE_COLS = 256
EXPERT_ROWS = 256
DISPATCH_ROWS = 256
COMBINE_ROWS = 256

F32 = jnp.float32
BF16 = jnp.bfloat16
U32 = jnp.uint32
I32 = jnp.int32


def _sigmoid(v):
    return 1.0 / (1.0 + jnp.exp(-v))


def _dot(a, b):
    return jnp.dot(a, b, preferred_element_type=F32)


def _dot_nt(a, b):
    return lax.dot_general(a, b, (((1,), (1,)), ((), ())), preferred_element_type=F32)


def _dot_tn(a, b):
    return lax.dot_general(a, b, (((0,), (0,)), ((), ())), preferred_element_type=F32)


ROW_TILE = 4
LANES = 128
HALF = ROW_TILE * LANES


def _store_packed_rows(ref, val):
    m = val.shape[0]
    for i in range(ROW_TILE):
        lo = val[:, i * LANES:(i + 1) * LANES]
        hi = val[:, HALF + i * LANES:HALF + (i + 1) * LANES]
        ref[pl.ds(i, m, stride=ROW_TILE), :] = pltpu.pack_elementwise([lo, hi], packed_dtype=BF16)


def _load_packed_chunks(ref, m):
    words = [ref[pl.ds(i, m, stride=ROW_TILE), :] for i in range(ROW_TILE)]
    unpack = lambda w, idx: pltpu.unpack_elementwise(w, index=idx, packed_dtype=BF16, unpacked_dtype=F32)
    return [unpack(w, 0) for w in words] + [unpack(w, 1) for w in words]


def _layer_norm(r, g, b):
    mu = jnp.mean(r, axis=-1, keepdims=True)
    c = r - mu
    var = jnp.mean(c * c, axis=-1, keepdims=True)
    return c * lax.rsqrt(var + LN_EPS) * g + b


def _proj_kernel(x_ref, wc_ref, wqkv_ref, wg_ref, bg_ref, wconv_ref, wco_ref,
                 cos_ref, sinlo_ref, sinhi_ref,
                 a_ref, gattn_ref, q_ref, k_ref, vt_ref, kmean_ref,
                 halo_ref, *, tiles_per_seq):
    tm = x_ref.shape[0]
    i = pl.program_id(0)
    xb = x_ref[...].astype(BF16)

    cb = _dot(xb, wc_ref[:, 0:CONV_WIDTH])
    cc = _dot(xb, wc_ref[:, CONV_WIDTH:2 * CONV_WIDTH])
    cv = _dot(xb, wc_ref[:, 2 * CONV_WIDTH:3 * CONV_WIDTH])
    u = cc * cv
    @pl.when((i % tiles_per_seq) == 0)
    def _():
        halo_ref[...] = jnp.zeros(halo_ref.shape, F32)

    prev = halo_ref[...]
    row = lax.broadcasted_iota(I32, u.shape, 0)
    u1 = jnp.where(row == 0, prev[7:8, :], pltpu.roll(u, 1, 0))
    u2 = jnp.where(row == 0, prev[6:7, :], jnp.where(row == 1, prev[7:8, :], pltpu.roll(u, 2, 0)))
    halo_ref[...] = u[tm - 8:tm, :]
    w0 = wconv_ref[0:1, :]
    w1 = wconv_ref[1:2, :]
    w2 = wconv_ref[2:3, :]
    yc = cb * (w0 * u2 + w1 * u1 + w2 * u)
    y_conv = _dot(yc.astype(BF16), wco_ref[...])

    g = _dot(xb, wg_ref[...]) + bg_ref[...]
    gates = _sigmoid(g)
    a_ref[...] = (gates[:, :D_MODEL] * y_conv).astype(a_ref.dtype)
    gattn_ref[...] = gates[:, D_MODEL:].astype(gattn_ref.dtype)

    qkv = _dot(xb, wqkv_ref[...])
    cos = cos_ref[...]
    sinlo = sinlo_ref[...]
    sinhi = sinhi_ref[...]

    def rope(t):
        return (t * cos + pltpu.roll(t, ATTN_WIDTH - ROPE_DIM // 2, 1) * sinlo
                + pltpu.roll(t, ROPE_DIM // 2, 1) * sinhi)

    q = rope(qkv[:, 0:ATTN_WIDTH])
    k = rope(qkv[:, ATTN_WIDTH:2 * ATTN_WIDTH])
    q_ref[...] = (q * QK_SCALE).astype(q_ref.dtype)
    k_ref[...] = k.astype(k_ref.dtype)
    v = qkv[:, 2 * ATTN_WIDTH:]
    for blk in range(tm // MOBA_BLOCK):
        rows = slice(blk * MOBA_BLOCK, (blk + 1) * MOBA_BLOCK)
        kmean_ref[blk] = jnp.mean(k[rows, :], axis=0, keepdims=True)
        vt_ref[blk] = v[rows, :].T.astype(vt_ref.dtype)


def _proj(x2, wc, wqkv, wg, bg, wconv, wco, cos, sinlo, sinhi, seq):
    t = x2.shape[0]
    tm = PROJ_ROWS
    tiles_per_seq = seq // tm
    const = lambda i: (0, 0)
    rows = lambda i: (i, 0)
    pos = lambda i: (i % tiles_per_seq, 0)
    return pl.pallas_call(
        functools.partial(_proj_kernel, tiles_per_seq=tiles_per_seq),
        name="proj",
        grid=(t // tm,),
        in_specs=[
            pl.BlockSpec((tm, D_MODEL), rows),
            pl.BlockSpec(wc.shape, const),
            pl.BlockSpec(wqkv.shape, const),
            pl.BlockSpec(wg.shape, const),
            pl.BlockSpec(bg.shape, const),
            pl.BlockSpec(wconv.shape, const),
            pl.BlockSpec(wco.shape, const),
            pl.BlockSpec((tm, ATTN_WIDTH), pos),
            pl.BlockSpec((tm, ATTN_WIDTH), pos),
            pl.BlockSpec((tm, ATTN_WIDTH), pos),
        ],
        out_specs=[
            pl.BlockSpec((tm, D_MODEL), rows),
            pl.BlockSpec((tm, D_MODEL), rows),
            pl.BlockSpec((tm, ATTN_WIDTH), rows),
            pl.BlockSpec((tm, ATTN_WIDTH), rows),
            pl.BlockSpec((tm // MOBA_BLOCK, ATTN_WIDTH, MOBA_BLOCK), lambda i: (i, 0, 0)),
            pl.BlockSpec((tm // MOBA_BLOCK, 1, ATTN_WIDTH), lambda i: (i, 0, 0)),
        ],
        out_shape=[
            jax.ShapeDtypeStruct((t, D_MODEL), BF16),
            jax.ShapeDtypeStruct((t, D_MODEL), BF16),
            jax.ShapeDtypeStruct((t, ATTN_WIDTH), BF16),
            jax.ShapeDtypeStruct((t, ATTN_WIDTH), BF16),
            jax.ShapeDtypeStruct((t // MOBA_BLOCK, ATTN_WIDTH, MOBA_BLOCK), BF16),
            jax.ShapeDtypeStruct((t // MOBA_BLOCK, 1, ATTN_WIDTH), F32),
        ],
        scratch_shapes=[pltpu.VMEM((8, CONV_WIDTH), F32)],
        compiler_params=pltpu.CompilerParams(
            dimension_semantics=("arbitrary",), vmem_limit_bytes=VMEM_LIMIT_BYTES),
    )(x2, wc, wqkv, wg, bg, wconv, wco, cos, sinlo, sinhi)


def _attn_kernel(q_ref, k_ref, vt_ref, kmean_ref, o_ref, bias_ref, m_ref, l_ref, alpha_ref, p_ref, acc_ref,
                 *, n_blocks):
    j = pl.program_id(1)
    blk_rows = MOBA_BLOCK
    key_i = lax.broadcasted_iota(I32, (blk_rows, blk_rows), 0)
    qry_i = lax.broadcasted_iota(I32, (blk_rows, blk_rows), 1)
    causal = key_i <= qry_i
    blk_i = lax.broadcasted_iota(I32, (n_blocks, blk_rows), 0)
    past = blk_i < j

    for h in range(N_HEADS):
        cols = slice(h * HEAD_DIM, (h + 1) * HEAD_DIM)
        qh = q_ref[:, cols]
        km = kmean_ref[:, 0, cols]
        gate = _dot_nt(km, qh.astype(F32))
        gate = jnp.where(past, gate, -jnp.inf)
        rank = jnp.zeros(gate.shape, F32)
        for m in range(n_blocks):
            gm = gate[m:m + 1, :]
            ahead = jnp.where(gm > gate, 1.0, jnp.where(gm == gate, jnp.where(blk_i > m, 1.0, 0.0), 0.0))
            rank = rank + ahead
        bias_ref[h] = jnp.where(past, jnp.where(rank < MOBA_TOPK, 0.0, NEG_BIG), NEG_BIG)

    def scores(n):
        r = pl.multiple_of(n * blk_rows, blk_rows)
        return [_dot_nt(k_ref[pl.ds(r, blk_rows), h * HEAD_DIM:(h + 1) * HEAD_DIM],
                        q_ref[:, h * HEAD_DIM:(h + 1) * HEAD_DIM]) for h in range(N_HEADS)]

    for h, s_raw in enumerate(scores(j)):
        cols = slice(h * HEAD_DIM, (h + 1) * HEAD_DIM)
        s = jnp.where(causal, s_raw, NEG_BIG)
        m0 = jnp.max(s, axis=0, keepdims=True)
        p = jnp.exp(s - m0)
        m_ref[h:h + 1, :] = m0
        l_ref[h:h + 1, :] = jnp.sum(p, axis=0, keepdims=True)
        p_ref[h] = p.astype(p_ref.dtype)
    for h in range(N_HEADS):
        cols = slice(h * HEAD_DIM, (h + 1) * HEAD_DIM)
        acc_ref[cols, :] = _dot(vt_ref[j, cols, :], p_ref[h])

    def body(n, carry):
        for h, s_raw in enumerate(scores(n)):
            sn = s_raw + bias_ref[h, pl.ds(n, 1), :]
            m_run = m_ref[h:h + 1, :]
            m_new = jnp.maximum(m_run, jnp.max(sn, axis=0, keepdims=True))
            alpha = jnp.exp(m_run - m_new)
            pn = jnp.exp(sn - m_new)
            m_ref[h:h + 1, :] = m_new
            l_ref[h:h + 1, :] = alpha * l_ref[h:h + 1, :] + jnp.sum(pn, axis=0, keepdims=True)
            alpha_ref[h:h + 1, :] = alpha
            p_ref[h] = pn.astype(p_ref.dtype)
        pvs = [_dot(vt_ref[n, h * HEAD_DIM:(h + 1) * HEAD_DIM, :], p_ref[h]) for h in range(N_HEADS)]
        for h in range(N_HEADS):
            cols = slice(h * HEAD_DIM, (h + 1) * HEAD_DIM)
            acc_ref[cols, :] = acc_ref[cols, :] * alpha_ref[h:h + 1, :] + pvs[h]
        return carry

    lax.fori_loop(0, j, body, 0)
    for h in range(N_HEADS):
        cols = slice(h * HEAD_DIM, (h + 1) * HEAD_DIM)
        acc_ref[cols, :] = acc_ref[cols, :] / l_ref[h:h + 1, :]
    o_ref[...] = acc_ref[...].T.astype(o_ref.dtype)


def _attention(q, k, vt, kmean, batch, seq):
    n_blocks = seq // MOBA_BLOCK
    return pl.pallas_call(
        functools.partial(_attn_kernel, n_blocks=n_blocks),
        name="attention",
        grid=(batch, n_blocks),
        in_specs=[
            pl.BlockSpec((MOBA_BLOCK, ATTN_WIDTH), lambda b, j: (b * n_blocks + j, 0)),
            pl.BlockSpec((seq, ATTN_WIDTH), lambda b, j: (b, 0)),
            pl.BlockSpec((n_blocks, ATTN_WIDTH, MOBA_BLOCK), lambda b, j: (b, 0, 0)),
            pl.BlockSpec((n_blocks, 1, ATTN_WIDTH), lambda b, j: (b, 0, 0)),
        ],
        out_specs=pl.BlockSpec((MOBA_BLOCK, ATTN_WIDTH), lambda b, j: (b * n_blocks + j, 0)),
        out_shape=jax.ShapeDtypeStruct(q.shape, BF16),
        scratch_shapes=[
            pltpu.VMEM((N_HEADS, n_blocks, MOBA_BLOCK), F32),
            pltpu.VMEM((N_HEADS, MOBA_BLOCK), F32),
            pltpu.VMEM((N_HEADS, MOBA_BLOCK), F32),
            pltpu.VMEM((N_HEADS, MOBA_BLOCK), F32),
            pltpu.VMEM((N_HEADS, MOBA_BLOCK, MOBA_BLOCK), BF16),
            pltpu.VMEM((ATTN_WIDTH, MOBA_BLOCK), F32),
        ],
        compiler_params=pltpu.CompilerParams(
            dimension_semantics=("arbitrary", "arbitrary"), vmem_limit_bytes=VMEM_LIMIT_BYTES),
    )(q, k, vt, kmean)


def _post_kernel(o_ref, a_ref, gattn_ref, x_ref, wao_ref, wmo_ref, g1_ref, b1_ref,
                 ws1_ref, ws3_ref, ws2_ref, wrt_ref,
                 base_ref, ht_ref, logit_ref):
    y_attn = _dot(o_ref[...], wao_ref[...])
    merged = a_ref[...].astype(F32) + gattn_ref[...].astype(F32) * y_attn
    mix = _dot(merged.astype(BF16), wmo_ref[...])
    h = _layer_norm(DN_ALPHA * x_ref[...] + mix, g1_ref[...], b1_ref[...])
    hb = h.astype(BF16)
    s1 = _dot(hb, ws1_ref[...])
    s3 = _dot(hb, ws3_ref[...])
    shared = _dot((s1 * _sigmoid(s1) * s3).astype(BF16), ws2_ref[...])
    base_ref[...] = DN_ALPHA * h + shared
    _store_packed_rows(ht_ref, h)
    logit_ref[...] = _dot_nt(wrt_ref[...], h)


def _post(o, a, gattn, x2, wao, wmo, g1, b1, ws1, ws3, ws2, wrt):
    t = x2.shape[0]
    tm = PROJ_ROWS
    const = lambda i: (0, 0)
    rows = lambda i: (i, 0)
    return pl.pallas_call(
        _post_kernel,
        name="post",
        grid=(t // tm,),
        in_specs=[
            pl.BlockSpec((tm, ATTN_WIDTH), rows),
            pl.BlockSpec((tm, D_MODEL), rows),
            pl.BlockSpec((tm, D_MODEL), rows),
            pl.BlockSpec((tm, D_MODEL), rows),
            pl.BlockSpec(wao.shape, const),
            pl.BlockSpec(wmo.shape, const),
            pl.BlockSpec(g1.shape, const),
            pl.BlockSpec(b1.shape, const),
            pl.BlockSpec(ws1.shape, const),
            pl.BlockSpec(ws3.shape, const),
            pl.BlockSpec(ws2.shape, const),
            pl.BlockSpec(wrt.shape, const),
        ],
        out_specs=[
            pl.BlockSpec((tm, D_MODEL), rows),
            pl.BlockSpec((tm * ROW_TILE, LANES), rows),
            pl.BlockSpec((N_EXPERTS, tm), lambda i: (0, i)),
        ],
        out_shape=[
            jax.ShapeDtypeStruct((t, D_MODEL), F32),
            jax.ShapeDtypeStruct((t * ROW_TILE, LANES), U32),
            jax.ShapeDtypeStruct((N_EXPERTS, t), F32),
        ],
        compiler_params=pltpu.CompilerParams(
            dimension_semantics=("arbitrary",), vmem_limit_bytes=VMEM_LIMIT_BYTES),
    )(o, a, gattn, x2, wao, wmo, g1, b1, ws1, ws3, ws2, wrt)


def _route_kernel(logit_ref, bias_ref, tri_ref, eid_ref, pos_ref, wt_ref, cnt_ref, run_ref):
    i = pl.program_id(0)
    tm = logit_ref.shape[1]
    per_group = N_EXPERTS // N_GROUPS

    @pl.when(i == 0)
    def _():
        run_ref[...] = jnp.zeros(run_ref.shape, F32)

    scores = _sigmoid(logit_ref[...])
    biased = scores + bias_ref[...]
    b3 = biased.reshape(N_GROUPS, per_group, tm)
    i3 = lax.broadcasted_iota(I32, b3.shape, 1)
    m1 = jnp.max(b3, axis=1)
    first = jnp.min(jnp.where(b3 == m1[:, None, :], i3, per_group), axis=1)
    m2 = jnp.max(jnp.where(i3 == first[:, None, :], -jnp.inf, b3), axis=1)
    gscore = m1 + m2
    g_i = lax.broadcasted_iota(I32, gscore.shape, 0)
    grank = jnp.zeros(gscore.shape, F32)
    for m in range(N_GROUPS):
        gm = gscore[m:m + 1, :]
        ahead = jnp.where(gm > gscore, 1.0, jnp.where(gm == gscore, jnp.where(g_i > m, 1.0, 0.0), 0.0))
        grank = grank + ahead
    gsel = grank < TOPK_GROUPS
    work = jnp.where(gsel[:, None, :], b3, -jnp.inf).reshape(N_EXPERTS, tm)

    e_i = lax.broadcasted_iota(I32, (N_EXPERTS, tm), 0)
    sel = jnp.zeros((N_EXPERTS, tm), F32)
    idxs = []
    scs = []
    for _k in range(TOP_K):
        mx = jnp.max(work, axis=0, keepdims=True)
        idx = jnp.min(jnp.where(work == mx, e_i, N_EXPERTS), axis=0, keepdims=True)
        hit = e_i == idx
        scs.append(jnp.sum(jnp.where(hit, scores, 0.0), axis=0, keepdims=True))
        sel = sel + jnp.where(hit, 1.0, 0.0)
        work = jnp.where(hit, -jnp.inf, work)
        idxs.append(idx)

    rank = _dot(sel.astype(BF16), tri_ref[...]) + run_ref[...]
    run_ref[...] = run_ref[...] + jnp.sum(sel, axis=1, keepdims=True)
    cnt_ref[...] = run_ref[...]

    wsum = scs[0]
    for kk in range(1, TOP_K):
        wsum = wsum + scs[kk]
    for kk in range(TOP_K):
        pos = jnp.sum(jnp.where(e_i == idxs[kk], rank, 0.0), axis=0, keepdims=True)
        eid_ref[kk:kk + 1, :] = idxs[kk]
        pos_ref[kk:kk + 1, :] = pos.astype(I32)
        wt_ref[kk:kk + 1, :] = scs[kk] / wsum * ROUTED_SCALE


def _route(logits_t, bias_col, tri):
    t = logits_t.shape[1]
    tm = ROUTE_COLS
    cols = lambda i: (0, i)
    const = lambda i: (0, 0)
    return pl.pallas_call(
        _route_kernel,
        name="route",
        grid=(t // tm,),
        in_specs=[
            pl.BlockSpec((N_EXPERTS, tm), cols),
            pl.BlockSpec((N_EXPERTS, 1), const),
            pl.BlockSpec((tm, tm), const),
        ],
        out_specs=[
            pl.BlockSpec((TOP_K, tm), cols),
            pl.BlockSpec((TOP_K, tm), cols),
            pl.BlockSpec((TOP_K, tm), cols),
            pl.BlockSpec((N_EXPERTS, 1), const),
        ],
        out_shape=[
            jax.ShapeDtypeStruct((TOP_K, t), I32),
            jax.ShapeDtypeStruct((TOP_K, t), I32),
            jax.ShapeDtypeStruct((TOP_K, t), F32),
            jax.ShapeDtypeStruct((N_EXPERTS, 1), F32),
        ],
        scratch_shapes=[pltpu.VMEM((N_EXPERTS, 1), F32)],
        compiler_params=pltpu.CompilerParams(
            dimension_semantics=("arbitrary",), vmem_limit_bytes=VMEM_LIMIT_BYTES),
    )(logits_t, bias_col, tri)


def _slots_kernel(eid_ref, pos_ref, pstart_ref, dest_ref):
    tm = eid_ref.shape[1]
    e_i = lax.broadcasted_iota(I32, (N_EXPERTS, tm), 0)
    pstart = pstart_ref[...]
    for kk in range(TOP_K):
        start = jnp.sum(jnp.where(e_i == eid_ref[kk:kk + 1, :], pstart, 0.0), axis=0, keepdims=True)
        dest_ref[kk:kk + 1, :] = start.astype(I32) + pos_ref[kk:kk + 1, :]


def _slots(eid, pos, pstart_col):
    t = eid.shape[1]
    tm = ROUTE_COLS
    cols = lambda i: (0, i)
    return pl.pallas_call(
        _slots_kernel,
        name="slots",
        grid=(t // tm,),
        in_specs=[
            pl.BlockSpec((TOP_K, tm), cols),
            pl.BlockSpec((TOP_K, tm), cols),
            pl.BlockSpec((N_EXPERTS, 1), lambda i: (0, 0)),
        ],
        out_specs=pl.BlockSpec((TOP_K, tm), cols),
        out_shape=jax.ShapeDtypeStruct((TOP_K, t), I32),
        compiler_params=pltpu.CompilerParams(dimension_semantics=("arbitrary",)),
    )(eid, pos, pstart_col)


def _row_tile_at(ref, row):
    return ref.at[pl.ds(pl.multiple_of(row * ROW_TILE, ROW_TILE), ROW_TILE)]


def _dispatch_kernel(dest_ref, pad_lo_ref, pad_hi_ref, nvalid_ref, ht_ref, xpad_ref,
                     stage_ref, zero_ref, sem, pad_sem, tail_sem, *, experts_per_step):
    i = pl.program_id(0)
    n = pl.num_programs(0)
    tm = DISPATCH_ROWS
    slot = i % 2

    def pad_range(step, q):
        e = step * experts_per_step + q
        ec = jnp.minimum(e, N_EXPERTS - 1)
        lo = pad_lo_ref[ec]
        return lo, jnp.where(e < N_EXPERTS, pad_hi_ref[ec], lo)

    def start_pad(step):
        for q in range(experts_per_step):
            lo, hi = pad_range(step, q)

            def start_zero(s, c):
                pltpu.make_async_copy(
                    _row_tile_at(zero_ref, s - lo), _row_tile_at(xpad_ref, s), pad_sem).start()
                return c

            lax.fori_loop(lo, hi, start_zero, 0)

    def wait_pad(step):
        for q in range(experts_per_step):
            lo, hi = pad_range(step, q)
            count = hi - lo
            p = EXPERT_ROWS // 2
            while p >= 1:
                @pl.when((count & p) != 0)
                def _(p=p):
                    pltpu.make_async_copy(
                        zero_ref.at[pl.ds(0, p * ROW_TILE)], xpad_ref.at[pl.ds(0, p * ROW_TILE)], pad_sem).wait()
                p //= 2

    def tail_copy(tile):
        rows = EXPERT_ROWS * ROW_TILE
        return pltpu.make_async_copy(
            zero_ref, xpad_ref.at[pl.ds(pl.multiple_of(tile * rows, rows), rows)], tail_sem)

    def start_tail(tile, c):
        tail_copy(tile).start()
        return c

    def wait_tail(tile, c):
        tail_copy(tile).wait()
        return c

    n_tiles = xpad_ref.shape[0] // (EXPERT_ROWS * ROW_TILE)

    @pl.when(i == 0)
    def _():
        zero_ref[...] = jnp.zeros(zero_ref.shape, zero_ref.dtype)
        lax.fori_loop(nvalid_ref[0], n_tiles, start_tail, 0)

    start_pad(i)

    stage_ref[slot] = ht_ref[...]

    def issue(r, carry):
        src = _row_tile_at(stage_ref.at[slot], r)
        for kk in range(TOP_K):
            pltpu.make_async_copy(
                src, _row_tile_at(xpad_ref, dest_ref[r * TOP_K + kk]), sem.at[slot]).start(priority=kk % 2)
        return carry

    lax.fori_loop(0, tm, issue, 0)

    def drain(s):
        for _kk in range(TOP_K):
            pltpu.make_async_copy(
                stage_ref.at[s], xpad_ref.at[pl.ds(0, tm * ROW_TILE)], sem.at[s]).wait()

    @pl.when(i > 0)
    def _():
        drain(1 - slot)
        wait_pad(i - 1)

    @pl.when(i == n - 1)
    def _():
        drain(slot)
        wait_pad(i)
        lax.fori_loop(nvalid_ref[0], n_tiles, wait_tail, 0)


def _dispatch(dest_flat, pad_lo, pad_hi, n_valid, ht, cap):
    t = ht.shape[0] // ROW_TILE
    tm = DISPATCH_ROWS
    experts_per_step = -(-N_EXPERTS // (t // tm))
    return pl.pallas_call(
        functools.partial(_dispatch_kernel, experts_per_step=experts_per_step),
        name="dispatch",
        grid=(t // tm,),
        in_specs=[
            pl.BlockSpec((tm * TOP_K,), lambda i: (i,), memory_space=pltpu.SMEM),
            pl.BlockSpec(memory_space=pltpu.SMEM),
            pl.BlockSpec(memory_space=pltpu.SMEM),
            pl.BlockSpec(memory_space=pltpu.SMEM),
            pl.BlockSpec((tm * ROW_TILE, LANES), lambda i: (i, 0)),
        ],
        out_specs=pl.BlockSpec(memory_space=pl.ANY),
        out_shape=jax.ShapeDtypeStruct((cap * ROW_TILE, LANES), ht.dtype),
        scratch_shapes=[
            pltpu.VMEM((2, tm * ROW_TILE, LANES), ht.dtype),
            pltpu.VMEM((EXPERT_ROWS * ROW_TILE, LANES), ht.dtype),
            pltpu.SemaphoreType.DMA((2,)),
            pltpu.SemaphoreType.DMA(()),
            pltpu.SemaphoreType.DMA(()),
        ],
        compiler_params=pltpu.CompilerParams(
            dimension_semantics=("arbitrary",), has_side_effects=True),
    )(dest_flat, pad_lo, pad_hi, n_valid, ht)


def _expert_kernel(texp_ref, nvalid_ref, first_ref, next_ref, par_ref,
                   x_ref, w1_hbm, w3_hbm, w2_hbm, y_ref,
                   w1f_ref, w3f_ref, w2f_ref, w1b_ref, w3b_ref, w2b_ref, sem):
    i = pl.program_id(0)
    tm = EXPERT_ROWS
    valid = i < nvalid_ref[0]

    def weight_copies(e, s):
        return (pltpu.make_async_copy(w1_hbm.at[e], w1f_ref.at[s], sem.at[s]),
                pltpu.make_async_copy(w3_hbm.at[e], w3f_ref.at[s], sem.at[s]),
                pltpu.make_async_copy(w2_hbm.at[e], w2f_ref.at[s], sem.at[s]))

    @pl.when(jnp.logical_and(valid, first_ref[i] == 1))
    def _():
        s = par_ref[i]

        @pl.when(i == 0)
        def _():
            for c in weight_copies(texp_ref[0], 0):
                c.start()

        for c in weight_copies(texp_ref[i], s):
            c.wait()
        nxt = next_ref[i]

        @pl.when(nxt >= 0)
        def _():
            for c in weight_copies(nxt, 1 - s):
                c.start()

        w1b_ref[...] = w1f_ref[s].astype(BF16)
        w3b_ref[...] = w3f_ref[s].astype(BF16)
        w2b_ref[...] = w2f_ref[s].astype(BF16)

    @pl.when(valid)
    def _():
        xb = jnp.concatenate([c.astype(BF16) for c in _load_packed_chunks(x_ref, tm)], axis=1)
        g = _dot(xb, w1b_ref[...])
        u = _dot(xb, w3b_ref[...])
        hid = (g * _sigmoid(g) * u).astype(BF16)
        _store_packed_rows(y_ref, _dot(hid, w2b_ref[...]))

    @pl.when(jnp.logical_not(valid))
    def _():
        y_ref[...] = jnp.zeros(y_ref.shape, y_ref.dtype)


def _experts(tile_expert, n_valid, tile_first, tile_next, tile_par, xpad, w1, w3, w2):
    tm = EXPERT_ROWS
    n_tiles = xpad.shape[0] // (tm * ROW_TILE)

    def x_map(i, te, nv, fi, nx, pa):
        return (jnp.minimum(i, jnp.maximum(nv[0] - 1, 0)), 0)

    grid_spec = pltpu.PrefetchScalarGridSpec(
        num_scalar_prefetch=5,
        grid=(n_tiles,),
        in_specs=[
            pl.BlockSpec((tm * ROW_TILE, LANES), x_map),
            pl.BlockSpec(memory_space=pl.ANY),
            pl.BlockSpec(memory_space=pl.ANY),
            pl.BlockSpec(memory_space=pl.ANY),
        ],
        out_specs=pl.BlockSpec((tm * ROW_TILE, LANES), lambda i, te, nv, fi, nx, pa: (i, 0)),
        scratch_shapes=[
            pltpu.VMEM((2, D_MODEL, EXPERT_DIM), F32),
            pltpu.VMEM((2, D_MODEL, EXPERT_DIM), F32),
            pltpu.VMEM((2, EXPERT_DIM, D_MODEL), F32),
            pltpu.VMEM((D_MODEL, EXPERT_DIM), BF16),
            pltpu.VMEM((D_MODEL, EXPERT_DIM), BF16),
            pltpu.VMEM((EXPERT_DIM, D_MODEL), BF16),
            pltpu.SemaphoreType.DMA((2,)),
        ],
    )
    return pl.pallas_call(
        _expert_kernel,
        name="experts",
        grid_spec=grid_spec,
        out_shape=jax.ShapeDtypeStruct(xpad.shape, xpad.dtype),
        compiler_params=pltpu.CompilerParams(
            dimension_semantics=("arbitrary",), vmem_limit_bytes=VMEM_LIMIT_BYTES),
    )(tile_expert, n_valid, tile_first, tile_next, tile_par, xpad, w1, w3, w2)


def _combine_kernel(dest_ref, dest_next_ref, wt_ref, base_ref, g2_ref, b2_ref, ypad_ref, out_ref, buf_ref, sem):
    i = pl.program_id(0)
    n = pl.num_programs(0)
    tm = COMBINE_ROWS
    slot = i % 2

    def issue_from(d_ref, s):
        def issue(r, carry):
            for kk in range(TOP_K):
                pltpu.make_async_copy(
                    _row_tile_at(ypad_ref, d_ref[r * TOP_K + kk]), _row_tile_at(buf_ref.at[s, kk], r),
                    sem.at[s]).start(priority=kk % 2)
            return carry
        lax.fori_loop(0, tm, issue, 0)

    @pl.when(i == 0)
    def _():
        issue_from(dest_ref, 0)

    @pl.when(i + 1 < n)
    def _():
        issue_from(dest_next_ref, 1 - slot)

    for kk in range(TOP_K):
        pltpu.make_async_copy(
            ypad_ref.at[pl.ds(0, tm * ROW_TILE)], buf_ref.at[slot, kk], sem.at[slot]).wait()

    wt = wt_ref[...]
    n_chunks = D_MODEL // LANES
    parts = [base_ref[:, c * LANES:(c + 1) * LANES] for c in range(n_chunks)]
    for kk in range(TOP_K):
        wk = wt[:, kk:kk + 1]
        chunks = _load_packed_chunks(buf_ref.at[slot, kk], tm)
        parts = [acc + wk * c for acc, c in zip(parts, chunks)]
    r = jnp.concatenate(parts, axis=1)
    out_ref[...] = _layer_norm(r, g2_ref[...], b2_ref[...])


def _combine(dest_flat, wt_tok, base, g2, b2, ypad):
    t = base.shape[0]
    tm = COMBINE_ROWS
    n = t // tm
    rows = lambda i: (i, 0)
    const = lambda i: (0, 0)
    return pl.pallas_call(
        _combine_kernel,
        name="combine",
        grid=(n,),
        in_specs=[
            pl.BlockSpec((tm * TOP_K,), lambda i: (i,), memory_space=pltpu.SMEM),
            pl.BlockSpec((tm * TOP_K,), lambda i: (jnp.minimum(i + 1, n - 1),), memory_space=pltpu.SMEM),
            pl.BlockSpec((tm, TOP_K), rows),
            pl.BlockSpec((tm, D_MODEL), rows),
            pl.BlockSpec(g2.shape, const),
            pl.BlockSpec(b2.shape, const),
            pl.BlockSpec(memory_space=pl.ANY),
        ],
        out_specs=pl.BlockSpec((tm, D_MODEL), rows),
        out_shape=jax.ShapeDtypeStruct((t, D_MODEL), F32),
        scratch_shapes=[
            pltpu.VMEM((2, TOP_K, tm * ROW_TILE, LANES), ypad.dtype),
            pltpu.SemaphoreType.DMA((2,)),
        ],
        compiler_params=pltpu.CompilerParams(
            dimension_semantics=("arbitrary",), vmem_limit_bytes=VMEM_LIMIT_BYTES),
    )(dest_flat, dest_flat, wt_tok, base, g2, b2, ypad)


def _rope_tables(seq):
    half = ROPE_DIM // 2
    inv = jnp.power(ROPE_THETA, -jnp.arange(half, dtype=F32) * 2.0 / ROPE_DIM)
    ang = jnp.arange(seq, dtype=F32)[:, None] * inv[None, :]
    cos = jnp.cos(ang)
    sin = jnp.sin(ang)
    ones = jnp.ones((seq, HEAD_DIM - ROPE_DIM), F32)
    zeros_rest = jnp.zeros((seq, HEAD_DIM - ROPE_DIM), F32)
    zeros_half = jnp.zeros((seq, half), F32)
    cos_h = jnp.concatenate([cos, cos, ones], axis=1)
    sinlo_h = jnp.concatenate([-sin, zeros_half, zeros_rest], axis=1)
    sinhi_h = jnp.concatenate([zeros_half, sin, zeros_rest], axis=1)
    tile = lambda m: jnp.tile(m, (1, N_HEADS))
    return tile(cos_h), tile(sinlo_h), tile(sinhi_h)


def kernel(x, w_in, b_gate, w_conv, w_conv_out, w_attn_out, w_mix_out, ln1_g, ln1_b, w_router, router_bias,
           w1, w3, w2, ws1, ws3, ws2, ln2_g, ln2_b):
    batch, seq, d = x.shape
    t = batch * seq
    assert d == D_MODEL and seq % PROJ_ROWS == 0 and seq % MOBA_BLOCK == 0
    x2 = x.reshape(t, d)
    h = None
    for layer in range(DEPTH):
        c3 = 3 * CONV_WIDTH
        a3 = 3 * ATTN_WIDTH
        w_in_b = w_in[layer].astype(BF16)
        wc = w_in_b[:, :c3]
        wqkv = w_in_b[:, c3:c3 + a3]
        wg = w_in_b[:, c3 + a3:]
        cos, sinlo, sinhi = _rope_tables(seq)
        a, gattn, q, k, v, kmean = _proj(
            x2, wc, wqkv, wg, b_gate[layer][None, :], w_conv[layer], w_conv_out[layer].astype(BF16),
            cos, sinlo, sinhi, seq)
        o = _attention(q, k, v, kmean, batch, seq)
        base, ht, logits_t = _post(
            o, a, gattn, x2, w_attn_out[layer].astype(BF16), w_mix_out[layer].astype(BF16),
            ln1_g[layer][None, :], ln1_b[layer][None, :],
            ws1[layer].astype(BF16), ws3[layer].astype(BF16), ws2[layer].astype(BF16),
            w_router[layer].T)

        tri = (jnp.arange(ROUTE_COLS)[:, None] < jnp.arange(ROUTE_COLS)[None, :]).astype(BF16)
        eid, pos, wts, counts = _route(logits_t, router_bias[layer][:, None], tri)

        cnt = counts[:, 0].astype(I32)
        padded = (cnt + EXPERT_ROWS - 1) // EXPERT_ROWS * EXPERT_ROWS
        pend = jnp.cumsum(padded)
        pstart = pend - padded
        cap = t * TOP_K + N_EXPERTS * EXPERT_ROWS
        n_tiles = cap // EXPERT_ROWS
        dest = _slots(eid, pos, pstart.astype(F32)[:, None])
        dest_flat = dest.T.reshape(t * TOP_K)
        tile_start = jnp.arange(n_tiles, dtype=I32) * EXPERT_ROWS
        tile_expert = jnp.minimum(
            jnp.sum((pend[None, :] <= tile_start[:, None]).astype(I32), axis=1), N_EXPERTS - 1)
        n_valid = (pend[-1:] // EXPERT_ROWS).astype(I32)
        tile_i = jnp.arange(n_tiles, dtype=I32)
        prev_expert = jnp.concatenate([jnp.full((1,), -1, I32), tile_expert[:-1]])
        tile_first = ((tile_i < n_valid[0]) & (tile_expert != prev_expert)).astype(I32)
        first_pos = jnp.where(tile_first == 1, tile_i, n_tiles)
        later_first = lax.cummin(jnp.concatenate([first_pos[1:], jnp.full((1,), n_tiles, I32)]), reverse=True)
        tile_next = jnp.where(later_first < n_tiles, tile_expert[jnp.minimum(later_first, n_tiles - 1)], -1)
        tile_par = (jnp.cumsum(tile_first) - 1) % 2

        xpad = _dispatch(dest_flat, pstart + cnt, pend, n_valid, ht, cap)
        ypad = _experts(tile_expert, n_valid, tile_first, tile_next.astype(I32), tile_par.astype(I32),
                        xpad, w1[layer], w3[layer], w2[layer])
        h = _combine(dest_flat, wts.T, base, ln2_g[layer][None, :], ln2_b[layer][None, :], ypad)
        x2 = h
    return h.reshape(batch, seq, d)
```

```python
import functools

import jax
import jax.numpy as jnp
from jax import lax
from jax.experimental import pallas as pl
from jax.experimental.pallas import tpu as pltpu

D_MODEL = 1024
CONV_WIDTH = 1024
CONV_K = 3
N_HEADS = 8
HEAD_DIM = 64
ATTN_WIDTH = N_HEADS * HEAD_DIM
MOBA_BLOCK = 256
MOBA_TOPK = 3
ROPE_THETA = 500000.0
ROPE_DIM = HEAD_DIM // 4
N_EXPERTS = 256
TOP_K = 8
N_GROUPS = 8
TOPK_GROUPS = 4
EXPERT_DIM = 256
SHARED_DIM = 256
ROUTED_SCALE = 2.5
DEPTH = 1
DN_ALPHA = (2 * DEPTH) ** 0.25
LN_EPS = 1e-5

NEG_BIG = -1e30
QK_SCALE = HEAD_DIM ** -0.5
assert QK_SCALE == 2.0 ** -3

VMEM_LIMIT_BYTES = 56 * 1024 * 1024

PROJ_ROWS = 512
ROUTE_COLS = 256
EXPERT_ROWS = 256
X_BUFFERS = 4
DISPATCH_ROWS = 256
COMBINE_ROWS = 256

F32 = jnp.float32
BF16 = jnp.bfloat16
U32 = jnp.uint32
I32 = jnp.int32


def _sigmoid(v):
    return 1.0 / (1.0 + jnp.exp(-v))


def _dot(a, b):
    return jnp.dot(a, b, preferred_element_type=F32)


def _dot_nt(a, b):
    return lax.dot_general(a, b, (((1,), (1,)), ((), ())), preferred_element_type=F32)


def _dot_tn(a, b):
    return lax.dot_general(a, b, (((0,), (0,)), ((), ())), preferred_element_type=F32)


ROW_TILE = 4
LANES = 128
HALF = ROW_TILE * LANES


def _store_packed_rows(ref, val):
    m = val.shape[0]
    for i in range(ROW_TILE):
        lo = val[:, i * LANES:(i + 1) * LANES]
        hi = val[:, HALF + i * LANES:HALF + (i + 1) * LANES]
        ref[pl.ds(i, m, stride=ROW_TILE), :] = pltpu.pack_elementwise([lo, hi], packed_dtype=BF16)


def _load_packed_chunks(ref, m):
    words = [ref[pl.ds(i, m, stride=ROW_TILE), :] for i in range(ROW_TILE)]
    unpack = lambda w, idx: pltpu.unpack_elementwise(w, index=idx, packed_dtype=BF16, unpacked_dtype=F32)
    return [unpack(w, 0) for w in words] + [unpack(w, 1) for w in words]


def _layer_norm(r, g, b):
    mu = jnp.mean(r, axis=-1, keepdims=True)
    c = r - mu
    var = jnp.mean(c * c, axis=-1, keepdims=True)
    return c * lax.rsqrt(var + LN_EPS) * g + b


def _proj_kernel(x_ref, wc_ref, wqkv_ref, wg_ref, bg_ref, wconv_ref, wco_ref,
                 cos_ref, sinlo_ref, sinhi_ref,
                 a_ref, gattn_ref, q_ref, k_ref, vt_ref, kmean_ref,
                 halo_ref, *, tiles_per_seq):
    tm = x_ref.shape[0]
    i = pl.program_id(0)
    xb = x_ref[...].astype(BF16)

    cb = _dot(xb, wc_ref[:, 0:CONV_WIDTH])
    cc = _dot(xb, wc_ref[:, CONV_WIDTH:2 * CONV_WIDTH])
    cv = _dot(xb, wc_ref[:, 2 * CONV_WIDTH:3 * CONV_WIDTH])
    u = cc * cv
    @pl.when((i % tiles_per_seq) == 0)
    def _():
        halo_ref[...] = jnp.zeros(halo_ref.shape, F32)

    prev = halo_ref[...]
    row = lax.broadcasted_iota(I32, u.shape, 0)
    u1 = jnp.where(row == 0, prev[7:8, :], pltpu.roll(u, 1, 0))
    u2 = jnp.where(row == 0, prev[6:7, :], jnp.where(row == 1, prev[7:8, :], pltpu.roll(u, 2, 0)))
    halo_ref[...] = u[tm - 8:tm, :]
    w0 = wconv_ref[0:1, :]
    w1 = wconv_ref[1:2, :]
    w2 = wconv_ref[2:3, :]
    yc = cb * (w0 * u2 + w1 * u1 + w2 * u)
    y_conv = _dot(yc.astype(BF16), wco_ref[...])

    g = _dot(xb, wg_ref[...]) + bg_ref[...]
    gates = _sigmoid(g)
    a_ref[...] = (gates[:, :D_MODEL] * y_conv).astype(a_ref.dtype)
    gattn_ref[...] = gates[:, D_MODEL:].astype(gattn_ref.dtype)

    qkv = _dot(xb, wqkv_ref[...])
    cos = cos_ref[...]
    sinlo = sinlo_ref[...]
    sinhi = sinhi_ref[...]

    def rope(t):
        return (t * cos + pltpu.roll(t, ATTN_WIDTH - ROPE_DIM // 2, 1) * sinlo
                + pltpu.roll(t, ROPE_DIM // 2, 1) * sinhi)

    q = rope(qkv[:, 0:ATTN_WIDTH])
    k = rope(qkv[:, ATTN_WIDTH:2 * ATTN_WIDTH])
    q_ref[...] = (q * QK_SCALE).astype(q_ref.dtype)
    k_ref[...] = k.astype(k_ref.dtype)
    v = qkv[:, 2 * ATTN_WIDTH:]
    for blk in range(tm // MOBA_BLOCK):
        rows = slice(blk * MOBA_BLOCK, (blk + 1) * MOBA_BLOCK)
        kmean_ref[blk] = jnp.mean(k[rows, :], axis=0, keepdims=True)
        vt_ref[blk] = v[rows, :].T.astype(vt_ref.dtype)


def _proj(x2, wc, wqkv, wg, bg, wconv, wco, cos, sinlo, sinhi, seq):
    t = x2.shape[0]
    tm = PROJ_ROWS
    tiles_per_seq = seq // tm
    const = lambda i: (0, 0)
    rows = lambda i: (i, 0)
    pos = lambda i: (i % tiles_per_seq, 0)
    return pl.pallas_call(
        functools.partial(_proj_kernel, tiles_per_seq=tiles_per_seq),
        name="proj",
        grid=(t // tm,),
        in_specs=[
            pl.BlockSpec((tm, D_MODEL), rows),
            pl.BlockSpec(wc.shape, const),
            pl.BlockSpec(wqkv.shape, const),
            pl.BlockSpec(wg.shape, const),
            pl.BlockSpec(bg.shape, const),
            pl.BlockSpec(wconv.shape, const),
            pl.BlockSpec(wco.shape, const),
            pl.BlockSpec((tm, ATTN_WIDTH), pos),
            pl.BlockSpec((tm, ATTN_WIDTH), pos),
            pl.BlockSpec((tm, ATTN_WIDTH), pos),
        ],
        out_specs=[
            pl.BlockSpec((tm, D_MODEL), rows),
            pl.BlockSpec((tm, D_MODEL), rows),
            pl.BlockSpec((tm, ATTN_WIDTH), rows),
            pl.BlockSpec((tm, ATTN_WIDTH), rows),
            pl.BlockSpec((tm // MOBA_BLOCK, ATTN_WIDTH, MOBA_BLOCK), lambda i: (i, 0, 0)),
            pl.BlockSpec((tm // MOBA_BLOCK, 1, ATTN_WIDTH), lambda i: (i, 0, 0)),
        ],
        out_shape=[
            jax.ShapeDtypeStruct((t, D_MODEL), BF16),
            jax.ShapeDtypeStruct((t, D_MODEL), BF16),
            jax.ShapeDtypeStruct((t, ATTN_WIDTH), BF16),
            jax.ShapeDtypeStruct((t, ATTN_WIDTH), BF16),
            jax.ShapeDtypeStruct((t // MOBA_BLOCK, ATTN_WIDTH, MOBA_BLOCK), BF16),
            jax.ShapeDtypeStruct((t // MOBA_BLOCK, 1, ATTN_WIDTH), F32),
        ],
        scratch_shapes=[pltpu.VMEM((8, CONV_WIDTH), F32)],
        compiler_params=pltpu.CompilerParams(
            dimension_semantics=("arbitrary",), vmem_limit_bytes=VMEM_LIMIT_BYTES),
    )(x2, wc, wqkv, wg, bg, wconv, wco, cos, sinlo, sinhi)


def _attn_kernel(q_ref, k_ref, vt_ref, kmean_ref, o_ref, bias_ref, m_ref, l_ref, alpha_ref, p_ref, acc_ref,
                 *, n_blocks):
    j = pl.program_id(1)
    blk_rows = MOBA_BLOCK
    key_i = lax.broadcasted_iota(I32, (blk_rows, blk_rows), 0)
    qry_i = lax.broadcasted_iota(I32, (blk_rows, blk_rows), 1)
    causal = key_i <= qry_i
    blk_i = lax.broadcasted_iota(I32, (n_blocks, blk_rows), 0)
    past = blk_i < j

    for h in range(N_HEADS):
        cols = slice(h * HEAD_DIM, (h + 1) * HEAD_DIM)
        qh = q_ref[:, cols]
        km = kmean_ref[:, 0, cols]
        gate = _dot_nt(km, qh.astype(F32))
        gate = jnp.where(past, gate, -jnp.inf)
        rank = jnp.zeros(gate.shape, F32)
        for m in range(n_blocks):
            gm = gate[m:m + 1, :]
            ahead = jnp.where(gm > gate, 1.0, jnp.where(gm == gate, jnp.where(blk_i > m, 1.0, 0.0), 0.0))
            rank = rank + ahead
        bias_ref[h] = jnp.where(past, jnp.where(rank < MOBA_TOPK, 0.0, NEG_BIG), NEG_BIG)

    def scores(n):
        r = pl.multiple_of(n * blk_rows, blk_rows)
        return [_dot_nt(k_ref[pl.ds(r, blk_rows), h * HEAD_DIM:(h + 1) * HEAD_DIM],
                        q_ref[:, h * HEAD_DIM:(h + 1) * HEAD_DIM]) for h in range(N_HEADS)]

    for h, s_raw in enumerate(scores(j)):
        cols = slice(h * HEAD_DIM, (h + 1) * HEAD_DIM)
        s = jnp.where(causal, s_raw, NEG_BIG)
        m0 = jnp.max(s, axis=0, keepdims=True)
        p = jnp.exp(s - m0)
        m_ref[h:h + 1, :] = m0
        l_ref[h:h + 1, :] = jnp.sum(p, axis=0, keepdims=True)
        p_ref[h] = p.astype(p_ref.dtype)
    for h in range(N_HEADS):
        cols = slice(h * HEAD_DIM, (h + 1) * HEAD_DIM)
        acc_ref[cols, :] = _dot(vt_ref[j, cols, :], p_ref[h])

    def body(n, carry):
        for h, s_raw in enumerate(scores(n)):
            sn = s_raw + bias_ref[h, pl.ds(n, 1), :]
            m_run = m_ref[h:h + 1, :]
            m_new = jnp.maximum(m_run, jnp.max(sn, axis=0, keepdims=True))
            alpha = jnp.exp(m_run - m_new)
            pn = jnp.exp(sn - m_new)
            m_ref[h:h + 1, :] = m_new
            l_ref[h:h + 1, :] = alpha * l_ref[h:h + 1, :] + jnp.sum(pn, axis=0, keepdims=True)
            alpha_ref[h:h + 1, :] = alpha
            p_ref[h] = pn.astype(p_ref.dtype)
        pvs = [_dot(vt_ref[n, h * HEAD_DIM:(h + 1) * HEAD_DIM, :], p_ref[h]) for h in range(N_HEADS)]
        for h in range(N_HEADS):
            cols = slice(h * HEAD_DIM, (h + 1) * HEAD_DIM)
            acc_ref[cols, :] = acc_ref[cols, :] * alpha_ref[h:h + 1, :] + pvs[h]
        return carry

    lax.fori_loop(0, j, body, 0)
    for h in range(N_HEADS):
        cols = slice(h * HEAD_DIM, (h + 1) * HEAD_DIM)
        acc_ref[cols, :] = acc_ref[cols, :] / l_ref[h:h + 1, :]
    o_ref[...] = acc_ref[...].T.astype(o_ref.dtype)


def _attention(q, k, vt, kmean, batch, seq):
    n_blocks = seq // MOBA_BLOCK
    return pl.pallas_call(
        functools.partial(_attn_kernel, n_blocks=n_blocks),
        name="attention",
        grid=(batch, n_blocks),
        in_specs=[
            pl.BlockSpec((MOBA_BLOCK, ATTN_WIDTH), lambda b, j: (b * n_blocks + j, 0)),
            pl.BlockSpec((seq, ATTN_WIDTH), lambda b, j: (b, 0)),
            pl.BlockSpec((n_blocks, ATTN_WIDTH, MOBA_BLOCK), lambda b, j: (b, 0, 0)),
            pl.BlockSpec((n_blocks, 1, ATTN_WIDTH), lambda b, j: (b, 0, 0)),
        ],
        out_specs=pl.BlockSpec((MOBA_BLOCK, ATTN_WIDTH), lambda b, j: (b * n_blocks + j, 0)),
        out_shape=jax.ShapeDtypeStruct(q.shape, BF16),
        scratch_shapes=[
            pltpu.VMEM((N_HEADS, n_blocks, MOBA_BLOCK), F32),
            pltpu.VMEM((N_HEADS, MOBA_BLOCK), F32),
            pltpu.VMEM((N_HEADS, MOBA_BLOCK), F32),
            pltpu.VMEM((N_HEADS, MOBA_BLOCK), F32),
            pltpu.VMEM((N_HEADS, MOBA_BLOCK, MOBA_BLOCK), BF16),
            pltpu.VMEM((ATTN_WIDTH, MOBA_BLOCK), F32),
        ],
        compiler_params=pltpu.CompilerParams(
            dimension_semantics=("arbitrary", "arbitrary"), vmem_limit_bytes=VMEM_LIMIT_BYTES),
    )(q, k, vt, kmean)


def _post_kernel(o_ref, a_ref, gattn_ref, x_ref, wao_ref, wmo_ref, g1_ref, b1_ref,
                 ws1_ref, ws3_ref, ws2_ref, wrt_ref,
                 base_ref, ht_ref, logit_ref):
    y_attn = _dot(o_ref[...], wao_ref[...])
    merged = a_ref[...].astype(F32) + gattn_ref[...].astype(F32) * y_attn
    mix = _dot(merged.astype(BF16), wmo_ref[...])
    h = _layer_norm(DN_ALPHA * x_ref[...] + mix, g1_ref[...], b1_ref[...])
    hb = h.astype(BF16)
    s1 = _dot(hb, ws1_ref[...])
    s3 = _dot(hb, ws3_ref[...])
    shared = _dot((s1 * _sigmoid(s1) * s3).astype(BF16), ws2_ref[...])
    base_ref[...] = DN_ALPHA * h + shared
    _store_packed_rows(ht_ref, h)
    logit_ref[...] = _dot_nt(wrt_ref[...], h)


def _post(o, a, gattn, x2, wao, wmo, g1, b1, ws1, ws3, ws2, wrt):
    t = x2.shape[0]
    tm = PROJ_ROWS
    const = lambda i: (0, 0)
    rows = lambda i: (i, 0)
    return pl.pallas_call(
        _post_kernel,
        name="post",
        grid=(t // tm,),
        in_specs=[
            pl.BlockSpec((tm, ATTN_WIDTH), rows),
            pl.BlockSpec((tm, D_MODEL), rows),
            pl.BlockSpec((tm, D_MODEL), rows),
            pl.BlockSpec((tm, D_MODEL), rows),
            pl.BlockSpec(wao.shape, const),
            pl.BlockSpec(wmo.shape, const),
            pl.BlockSpec(g1.shape, const),
            pl.BlockSpec(b1.shape, const),
            pl.BlockSpec(ws1.shape, const),
            pl.BlockSpec(ws3.shape, const),
            pl.BlockSpec(ws2.shape, const),
            pl.BlockSpec(wrt.shape, const),
        ],
        out_specs=[
            pl.BlockSpec((tm, D_MODEL), rows),
            pl.BlockSpec((tm * ROW_TILE, LANES), rows),
            pl.BlockSpec((N_EXPERTS, tm), lambda i: (0, i)),
        ],
        out_shape=[
            jax.ShapeDtypeStruct((t, D_MODEL), F32),
            jax.ShapeDtypeStruct((t * ROW_TILE, LANES), U32),
            jax.ShapeDtypeStruct((N_EXPERTS, t), F32),
        ],
        compiler_params=pltpu.CompilerParams(
            dimension_semantics=("arbitrary",), vmem_limit_bytes=VMEM_LIMIT_BYTES),
    )(o, a, gattn, x2, wao, wmo, g1, b1, ws1, ws3, ws2, wrt)


def _route_kernel(logit_ref, bias_ref, tri_ref, eid_ref, pos_ref, wt_ref, cnt_ref, run_ref):
    i = pl.program_id(0)
    tm = logit_ref.shape[1]
    per_group = N_EXPERTS // N_GROUPS

    @pl.when(i == 0)
    def _():
        run_ref[...] = jnp.zeros(run_ref.shape, F32)

    scores = _sigmoid(logit_ref[...])
    biased = scores + bias_ref[...]
    b3 = biased.reshape(N_GROUPS, per_group, tm)
    i3 = lax.broadcasted_iota(I32, b3.shape, 1)
    m1 = jnp.max(b3, axis=1)
    first = jnp.min(jnp.where(b3 == m1[:, None, :], i3, per_group), axis=1)
    m2 = jnp.max(jnp.where(i3 == first[:, None, :], -jnp.inf, b3), axis=1)
    gscore = m1 + m2
    g_i = lax.broadcasted_iota(I32, gscore.shape, 0)
    grank = jnp.zeros(gscore.shape, F32)
    for m in range(N_GROUPS):
        gm = gscore[m:m + 1, :]
        ahead = jnp.where(gm > gscore, 1.0, jnp.where(gm == gscore, jnp.where(g_i > m, 1.0, 0.0), 0.0))
        grank = grank + ahead
    gsel = grank < TOPK_GROUPS
    work = jnp.where(gsel[:, None, :], b3, -jnp.inf).reshape(N_EXPERTS, tm)

    e_i = lax.broadcasted_iota(I32, (N_EXPERTS, tm), 0)
    sel = jnp.zeros((N_EXPERTS, tm), F32)
    idxs = []
    scs = []
    for _k in range(TOP_K):
        mx = jnp.max(work, axis=0, keepdims=True)
        idx = jnp.min(jnp.where(work == mx, e_i, N_EXPERTS), axis=0, keepdims=True)
        hit = e_i == idx
        scs.append(jnp.sum(jnp.where(hit, scores, 0.0), axis=0, keepdims=True))
        sel = sel + jnp.where(hit, 1.0, 0.0)
        work = jnp.where(hit, -jnp.inf, work)
        idxs.append(idx)

    rank = _dot(sel.astype(BF16), tri_ref[...]) + run_ref[...]
    run_ref[...] = run_ref[...] + jnp.sum(sel, axis=1, keepdims=True)
    cnt_ref[...] = run_ref[...]

    wsum = scs[0]
    for kk in range(1, TOP_K):
        wsum = wsum + scs[kk]
    for kk in range(TOP_K):
        pos = jnp.sum(jnp.where(e_i == idxs[kk], rank, 0.0), axis=0, keepdims=True)
        eid_ref[kk:kk + 1, :] = idxs[kk]
        pos_ref[kk:kk + 1, :] = pos.astype(I32)
        wt_ref[kk:kk + 1, :] = scs[kk] / wsum * ROUTED_SCALE


def _route(logits_t, bias_col, tri):
    t = logits_t.shape[1]
    tm = ROUTE_COLS
    cols = lambda i: (0, i)
    const = lambda i: (0, 0)
    return pl.pallas_call(
        _route_kernel,
        name="route",
        grid=(t // tm,),
        in_specs=[
            pl.BlockSpec((N_EXPERTS, tm), cols),
            pl.BlockSpec((N_EXPERTS, 1), const),
            pl.BlockSpec((tm, tm), const),
        ],
        out_specs=[
            pl.BlockSpec((TOP_K, tm), cols),
            pl.BlockSpec((TOP_K, tm), cols),
            pl.BlockSpec((TOP_K, tm), cols),
            pl.BlockSpec((N_EXPERTS, 1), const),
        ],
        out_shape=[
            jax.ShapeDtypeStruct((TOP_K, t), I32),
            jax.ShapeDtypeStruct((TOP_K, t), I32),
            jax.ShapeDtypeStruct((TOP_K, t), F32),
            jax.ShapeDtypeStruct((N_EXPERTS, 1), F32),
        ],
        scratch_shapes=[pltpu.VMEM((N_EXPERTS, 1), F32)],
        compiler_params=pltpu.CompilerParams(
            dimension_semantics=("arbitrary",), vmem_limit_bytes=VMEM_LIMIT_BYTES),
    )(logits_t, bias_col, tri)


def _slots_kernel(eid_ref, pos_ref, pstart_ref, dest_ref):
    tm = eid_ref.shape[1]
    e_i = lax.broadcasted_iota(I32, (N_EXPERTS, tm), 0)
    pstart = pstart_ref[...]
    for kk in range(TOP_K):
        start = jnp.sum(jnp.where(e_i == eid_ref[kk:kk + 1, :], pstart, 0.0), axis=0, keepdims=True)
        dest_ref[kk:kk + 1, :] = start.astype(I32) + pos_ref[kk:kk + 1, :]


def _slots(eid, pos, pstart_col):
    t = eid.shape[1]
    tm = ROUTE_COLS
    cols = lambda i: (0, i)
    return pl.pallas_call(
        _slots_kernel,
        name="slots",
        grid=(t // tm,),
        in_specs=[
            pl.BlockSpec((TOP_K, tm), cols),
            pl.BlockSpec((TOP_K, tm), cols),
            pl.BlockSpec((N_EXPERTS, 1), lambda i: (0, 0)),
        ],
        out_specs=pl.BlockSpec((TOP_K, tm), cols),
        out_shape=jax.ShapeDtypeStruct((TOP_K, t), I32),
        compiler_params=pltpu.CompilerParams(dimension_semantics=("arbitrary",)),
    )(eid, pos, pstart_col)


def _row_tile_at(ref, row):
    return ref.at[pl.ds(pl.multiple_of(row * ROW_TILE, ROW_TILE), ROW_TILE)]


def _dispatch_kernel(dest_ref, pad_lo_ref, pad_hi_ref, nvalid_ref, ht_ref, xpad_ref,
                     stage_ref, zero_ref, sem, pad_sem, tail_sem, *, experts_per_step):
    i = pl.program_id(0)
    n = pl.num_programs(0)
    tm = DISPATCH_ROWS
    slot = i % 2

    def pad_range(step, q):
        e = step * experts_per_step + q
        ec = jnp.minimum(e, N_EXPERTS - 1)
        lo = pad_lo_ref[ec]
        return lo, jnp.where(e < N_EXPERTS, pad_hi_ref[ec], lo)

    def start_pad(step):
        for q in range(experts_per_step):
            lo, hi = pad_range(step, q)

            def start_zero(s, c):
                pltpu.make_async_copy(
                    _row_tile_at(zero_ref, s - lo), _row_tile_at(xpad_ref, s), pad_sem).start()
                return c

            lax.fori_loop(lo, hi, start_zero, 0)

    def wait_pad(step):
        for q in range(experts_per_step):
            lo, hi = pad_range(step, q)
            count = hi - lo
            p = EXPERT_ROWS // 2
            while p >= 1:
                @pl.when((count & p) != 0)
                def _(p=p):
                    pltpu.make_async_copy(
                        zero_ref.at[pl.ds(0, p * ROW_TILE)], xpad_ref.at[pl.ds(0, p * ROW_TILE)], pad_sem).wait()
                p //= 2

    def tail_copy(tile):
        rows = EXPERT_ROWS * ROW_TILE
        return pltpu.make_async_copy(
            zero_ref, xpad_ref.at[pl.ds(pl.multiple_of(tile * rows, rows), rows)], tail_sem)

    def start_tail(tile, c):
        tail_copy(tile).start()
        return c

    def wait_tail(tile, c):
        tail_copy(tile).wait()
        return c

    n_tiles = xpad_ref.shape[0] // (EXPERT_ROWS * ROW_TILE)

    @pl.when(i == 0)
    def _():
        zero_ref[...] = jnp.zeros(zero_ref.shape, zero_ref.dtype)
        lax.fori_loop(nvalid_ref[0], n_tiles, start_tail, 0)

    start_pad(i)

    stage_ref[slot] = ht_ref[...]

    def issue(r, carry):
        src = _row_tile_at(stage_ref.at[slot], r)
        for kk in range(TOP_K):
            pltpu.make_async_copy(
                src, _row_tile_at(xpad_ref, dest_ref[r * TOP_K + kk]), sem.at[slot]).start(priority=kk % 2)
        return carry

    lax.fori_loop(0, tm, issue, 0)

    def drain(s):
        for _kk in range(TOP_K):
            pltpu.make_async_copy(
                stage_ref.at[s], xpad_ref.at[pl.ds(0, tm * ROW_TILE)], sem.at[s]).wait()

    @pl.when(i > 0)
    def _():
        drain(1 - slot)
        wait_pad(i - 1)

    @pl.when(i == n - 1)
    def _():
        drain(slot)
        wait_pad(i)
        lax.fori_loop(nvalid_ref[0], n_tiles, wait_tail, 0)


def _dispatch(dest_flat, pad_lo, pad_hi, n_valid, ht, cap):
    t = ht.shape[0] // ROW_TILE
    tm = DISPATCH_ROWS
    experts_per_step = -(-N_EXPERTS // (t // tm))
    return pl.pallas_call(
        functools.partial(_dispatch_kernel, experts_per_step=experts_per_step),
        name="dispatch",
        grid=(t // tm,),
        in_specs=[
            pl.BlockSpec((tm * TOP_K,), lambda i: (i,), memory_space=pltpu.SMEM),
            pl.BlockSpec(memory_space=pltpu.SMEM),
            pl.BlockSpec(memory_space=pltpu.SMEM),
            pl.BlockSpec(memory_space=pltpu.SMEM),
            pl.BlockSpec((tm * ROW_TILE, LANES), lambda i: (i, 0)),
        ],
        out_specs=pl.BlockSpec(memory_space=pl.ANY),
        out_shape=jax.ShapeDtypeStruct((cap * ROW_TILE, LANES), ht.dtype),
        scratch_shapes=[
            pltpu.VMEM((2, tm * ROW_TILE, LANES), ht.dtype),
            pltpu.VMEM((EXPERT_ROWS * ROW_TILE, LANES), ht.dtype),
            pltpu.SemaphoreType.DMA((2,)),
            pltpu.SemaphoreType.DMA(()),
            pltpu.SemaphoreType.DMA(()),
        ],
        compiler_params=pltpu.CompilerParams(
            dimension_semantics=("arbitrary",), has_side_effects=True),
    )(dest_flat, pad_lo, pad_hi, n_valid, ht)


def _expert_kernel(texp_ref, nvalid_ref, first_ref, next_ref, par_ref,
                   x_hbm, w1_hbm, w3_hbm, w2_hbm, y_ref,
                   w1f_ref, w3f_ref, w2f_ref, w1b_ref, w3b_ref, w2b_ref, hid_ref, xbuf_ref, sem, xsem):
    i = pl.program_id(0)
    n_tiles = pl.num_programs(0) - 1
    tm = EXPERT_ROWS
    cur = jnp.minimum(i, n_tiles - 1)
    prev = jnp.maximum(i - 1, 0)
    cur_valid = i < nvalid_ref[0]
    prev_valid = jnp.logical_and(i >= 1, i - 1 < nvalid_ref[0])

    def x_copy(tile):
        rows = tm * ROW_TILE
        return pltpu.make_async_copy(
            x_hbm.at[pl.ds(pl.multiple_of(tile * rows, rows), rows)],
            xbuf_ref.at[tile % X_BUFFERS], xsem.at[tile % X_BUFFERS])

    @pl.when(i == 0)
    def _():
        for t in range(X_BUFFERS - 1):
            @pl.when(t < nvalid_ref[0])
            def _(t=t):
                x_copy(t).start()

    @pl.when(i + (X_BUFFERS - 1) < nvalid_ref[0])
    def _():
        x_copy(i + (X_BUFFERS - 1)).start()

    @pl.when(cur_valid)
    def _():
        x_copy(i).wait()

    def weight_copies(e, s):
        return (pltpu.make_async_copy(w1_hbm.at[e], w1f_ref.at[s], sem.at[s]),
                pltpu.make_async_copy(w3_hbm.at[e], w3f_ref.at[s], sem.at[s]),
                pltpu.make_async_copy(w2_hbm.at[e], w2f_ref.at[s], sem.at[s]))

    @pl.when(jnp.logical_and(cur_valid, first_ref[cur] == 1))
    def _():
        s = par_ref[cur]

        @pl.when(i == 0)
        def _():
            for c in weight_copies(texp_ref[0], 0):
                c.start()

        for c in weight_copies(texp_ref[cur], s):
            c.wait()
        nxt = next_ref[cur]

        @pl.when(nxt >= 0)
        def _():
            for c in weight_copies(nxt, 1 - s):
                c.start()

        w1b_ref[...] = w1f_ref[s].astype(BF16)
        w3b_ref[...] = w3f_ref[s].astype(BF16)
        w2b_ref[s] = w2f_ref[s].astype(BF16)

    def up_matmuls():
        chunks = _load_packed_chunks(xbuf_ref.at[i % X_BUFFERS], tm)
        xb = jnp.concatenate([c.astype(BF16) for c in chunks], axis=1)
        return _dot(xb, w1b_ref[...]), _dot(xb, w3b_ref[...])

    def down_matmul():
        return _dot(hid_ref[(i - 1) % 2], w2b_ref[par_ref[prev]])

    def store_hidden(g, u):
        hid_ref[i % 2] = (g * _sigmoid(g) * u).astype(BF16)

    @pl.when(jnp.logical_and(cur_valid, prev_valid))
    def _():
        g, u = up_matmuls()
        y = down_matmul()
        store_hidden(g, u)
        _store_packed_rows(y_ref, y)

    @pl.when(jnp.logical_and(cur_valid, jnp.logical_not(prev_valid)))
    def _():
        store_hidden(*up_matmuls())

    @pl.when(jnp.logical_and(jnp.logical_not(cur_valid), prev_valid))
    def _():
        _store_packed_rows(y_ref, down_matmul())

    @pl.when(jnp.logical_and(i >= 1, jnp.logical_not(prev_valid)))
    def _():
        y_ref[...] = jnp.zeros(y_ref.shape, y_ref.dtype)


def _experts(tile_expert, n_valid, tile_first, tile_next, tile_par, xpad, w1, w3, w2):
    tm = EXPERT_ROWS
    n_tiles = xpad.shape[0] // (tm * ROW_TILE)

    grid_spec = pltpu.PrefetchScalarGridSpec(
        num_scalar_prefetch=5,
        grid=(n_tiles + 1,),
        in_specs=[
            pl.BlockSpec(memory_space=pl.ANY),
            pl.BlockSpec(memory_space=pl.ANY),
            pl.BlockSpec(memory_space=pl.ANY),
            pl.BlockSpec(memory_space=pl.ANY),
        ],
        out_specs=pl.BlockSpec((tm * ROW_TILE, LANES), lambda i, te, nv, fi, nx, pa: (jnp.maximum(i - 1, 0), 0)),
        scratch_shapes=[
            pltpu.VMEM((2, D_MODEL, EXPERT_DIM), F32),
            pltpu.VMEM((2, D_MODEL, EXPERT_DIM), F32),
            pltpu.VMEM((2, EXPERT_DIM, D_MODEL), F32),
            pltpu.VMEM((D_MODEL, EXPERT_DIM), BF16),
            pltpu.VMEM((D_MODEL, EXPERT_DIM), BF16),
            pltpu.VMEM((2, EXPERT_DIM, D_MODEL), BF16),
            pltpu.VMEM((2, tm, EXPERT_DIM), BF16),
            pltpu.VMEM((X_BUFFERS, tm * ROW_TILE, LANES), xpad.dtype),
            pltpu.SemaphoreType.DMA((2,)),
            pltpu.SemaphoreType.DMA((X_BUFFERS,)),
        ],
    )
    return pl.pallas_call(
        _expert_kernel,
        name="experts",
        grid_spec=grid_spec,
        out_shape=jax.ShapeDtypeStruct(xpad.shape, xpad.dtype),
        compiler_params=pltpu.CompilerParams(
            dimension_semantics=("arbitrary",), vmem_limit_bytes=VMEM_LIMIT_BYTES),
    )(tile_expert, n_valid, tile_first, tile_next, tile_par, xpad, w1, w3, w2)


def _combine_kernel(dest_ref, dest_next_ref, wt_ref, base_ref, g2_ref, b2_ref, ypad_ref, out_ref, buf_ref, sem):
    i = pl.program_id(0)
    n = pl.num_programs(0)
    tm = COMBINE_ROWS
    slot = i % 2

    def issue_from(d_ref, s):
        def issue(r, carry):
            for kk in range(TOP_K):
                pltpu.make_async_copy(
                    _row_tile_at(ypad_ref, d_ref[r * TOP_K + kk]), _row_tile_at(buf_ref.at[s, kk], r),
                    sem.at[s]).start(priority=kk % 2)
            return carry
        lax.fori_loop(0, tm, issue, 0)

    @pl.when(i == 0)
    def _():
        issue_from(dest_ref, 0)

    @pl.when(i + 1 < n)
    def _():
        issue_from(dest_next_ref, 1 - slot)

    for kk in range(TOP_K):
        pltpu.make_async_copy(
            ypad_ref.at[pl.ds(0, tm * ROW_TILE)], buf_ref.at[slot, kk], sem.at[slot]).wait()

    wt = wt_ref[...]
    n_chunks = D_MODEL // LANES
    parts = [base_ref[:, c * LANES:(c + 1) * LANES] for c in range(n_chunks)]
    for kk in range(TOP_K):
        wk = wt[:, kk:kk + 1]
        chunks = _load_packed_chunks(buf_ref.at[slot, kk], tm)
        parts = [acc + wk * c for acc, c in zip(parts, chunks)]
    r = jnp.concatenate(parts, axis=1)
    out_ref[...] = _layer_norm(r, g2_ref[...], b2_ref[...])


def _combine(dest_flat, wt_tok, base, g2, b2, ypad):
    t = base.shape[0]
    tm = COMBINE_ROWS
    n = t // tm
    rows = lambda i: (i, 0)
    const = lambda i: (0, 0)
    return pl.pallas_call(
        _combine_kernel,
        name="combine",
        grid=(n,),
        in_specs=[
            pl.BlockSpec((tm * TOP_K,), lambda i: (i,), memory_space=pltpu.SMEM),
            pl.BlockSpec((tm * TOP_K,), lambda i: (jnp.minimum(i + 1, n - 1),), memory_space=pltpu.SMEM),
            pl.BlockSpec((tm, TOP_K), rows),
            pl.BlockSpec((tm, D_MODEL), rows),
            pl.BlockSpec(g2.shape, const),
            pl.BlockSpec(b2.shape, const),
            pl.BlockSpec(memory_space=pl.ANY),
        ],
        out_specs=pl.BlockSpec((tm, D_MODEL), rows),
        out_shape=jax.ShapeDtypeStruct((t, D_MODEL), F32),
        scratch_shapes=[
            pltpu.VMEM((2, TOP_K, tm * ROW_TILE, LANES), ypad.dtype),
            pltpu.SemaphoreType.DMA((2,)),
        ],
        compiler_params=pltpu.CompilerParams(
            dimension_semantics=("arbitrary",), vmem_limit_bytes=VMEM_LIMIT_BYTES),
    )(dest_flat, dest_flat, wt_tok, base, g2, b2, ypad)


def _rope_tables(seq):
    half = ROPE_DIM // 2
    inv = jnp.power(ROPE_THETA, -jnp.arange(half, dtype=F32) * 2.0 / ROPE_DIM)
    ang = jnp.arange(seq, dtype=F32)[:, None] * inv[None, :]
    cos = jnp.cos(ang)
    sin = jnp.sin(ang)
    ones = jnp.ones((seq, HEAD_DIM - ROPE_DIM), F32)
    zeros_rest = jnp.zeros((seq, HEAD_DIM - ROPE_DIM), F32)
    zeros_half = jnp.zeros((seq, half), F32)
    cos_h = jnp.concatenate([cos, cos, ones], axis=1)
    sinlo_h = jnp.concatenate([-sin, zeros_half, zeros_rest], axis=1)
    sinhi_h = jnp.concatenate([zeros_half, sin, zeros_rest], axis=1)
    tile = lambda m: jnp.tile(m, (1, N_HEADS))
    return tile(cos_h), tile(sinlo_h), tile(sinhi_h)


def kernel(x, w_in, b_gate, w_conv, w_conv_out, w_attn_out, w_mix_out, ln1_g, ln1_b, w_router, router_bias,
           w1, w3, w2, ws1, ws3, ws2, ln2_g, ln2_b):
    batch, seq, d = x.shape
    t = batch * seq
    assert d == D_MODEL and seq % PROJ_ROWS == 0 and seq % MOBA_BLOCK == 0
    x2 = x.reshape(t, d)
    h = None
    for layer in range(DEPTH):
        c3 = 3 * CONV_WIDTH
        a3 = 3 * ATTN_WIDTH
        w_in_b = w_in[layer].astype(BF16)
        wc = w_in_b[:, :c3]
        wqkv = w_in_b[:, c3:c3 + a3]
        wg = w_in_b[:, c3 + a3:]
        cos, sinlo, sinhi = _rope_tables(seq)
        a, gattn, q, k, v, kmean = _proj(
            x2, wc, wqkv, wg, b_gate[layer][None, :], w_conv[layer], w_conv_out[layer].astype(BF16),
            cos, sinlo, sinhi, seq)
        o = _attention(q, k, v, kmean, batch, seq)
        base, ht, logits_t = _post(
            o, a, gattn, x2, w_attn_out[layer].astype(BF16), w_mix_out[layer].astype(BF16),
            ln1_g[layer][None, :], ln1_b[layer][None, :],
            ws1[layer].astype(BF16), ws3[layer].astype(BF16), ws2[layer].astype(BF16),
            w_router[layer].T)

        tri = (jnp.arange(ROUTE_COLS)[:, None] < jnp.arange(ROUTE_COLS)[None, :]).astype(BF16)
        eid, pos, wts, counts = _route(logits_t, router_bias[layer][:, None], tri)

        cnt = counts[:, 0].astype(I32)
        padded = (cnt + EXPERT_ROWS - 1) // EXPERT_ROWS * EXPERT_ROWS
        pend = jnp.cumsum(padded)
        pstart = pend - padded
        cap = t * TOP_K + N_EXPERTS * EXPERT_ROWS
        n_tiles = cap // EXPERT_ROWS
        dest = _slots(eid, pos, pstart.astype(F32)[:, None])
        dest_flat = dest.T.reshape(t * TOP_K)
        tile_start = jnp.arange(n_tiles, dtype=I32) * EXPERT_ROWS
        tile_expert = jnp.minimum(
            jnp.sum((pend[None, :] <= tile_start[:, None]).astype(I32), axis=1), N_EXPERTS - 1)
        n_valid = (pend[-1:] // EXPERT_ROWS).astype(I32)
        tile_i = jnp.arange(n_tiles, dtype=I32)
        prev_expert = jnp.concatenate([jnp.full((1,), -1, I32), tile_expert[:-1]])
        tile_first = ((tile_i < n_valid[0]) & (tile_expert != prev_expert)).astype(I32)
        first_pos = jnp.where(tile_first == 1, tile_i, n_tiles)
        later_first = lax.cummin(jnp.concatenate([first_pos[1:], jnp.full((1,), n_tiles, I32)]), reverse=True)
        tile_next = jnp.where(later_first < n_tiles, tile_expert[jnp.minimum(later_first, n_tiles - 1)], -1)
        tile_par = (jnp.cumsum(tile_first) - 1) % 2

        xpad = _dispatch(dest_flat, pstart + cnt, pend, n_valid, ht, cap)
        ypad = _experts(tile_expert, n_valid, tile_first, tile_next.astype(I32), tile_par.astype(I32),
                        xpad, w1[layer], w3[layer], w2[layer])
        h = _combine(dest_flat, wts.T, base, ln2_g[layer][None, :], ln2_b[layer][None, :], ypad)
        x2 = h
    return h.reshape(batch, seq, d)
```

```python
import functools

import jax
import jax.numpy as jnp
from jax import lax
from jax.experimental import pallas as pl
from jax.experimental.pallas import tpu as pltpu

D_MODEL = 1024
CONV_WIDTH = 1024
CONV_K = 3
N_HEADS = 8
HEAD_DIM = 64
ATTN_WIDTH = N_HEADS * HEAD_DIM
MOBA_BLOCK = 256
MOBA_TOPK = 3
ROPE_THETA = 500000.0
ROPE_DIM = HEAD_DIM // 4
N_EXPERTS = 256
TOP_K = 8
N_GROUPS = 8
TOPK_GROUPS = 4
EXPERT_DIM = 256
SHARED_DIM = 256
ROUTED_SCALE = 2.5
DEPTH = 1
DN_ALPHA = (2 * DEPTH) ** 0.25
LN_EPS = 1e-5

NEG_BIG = -1e30
QK_SCALE = HEAD_DIM ** -0.5
assert QK_SCALE == 2.0 ** -3

VMEM_LIMIT_BYTES = 56 * 1024 * 1024

PROJ_ROWS = 512
ROUTE_COLS = 256
SLOT_COLS = 1024
EXPERT_ROWS = 256
X_BUFFERS = 4
DISPATCH_ROWS = 256
COMBINE_ROWS = 256

F32 = jnp.float32
BF16 = jnp.bfloat16
U32 = jnp.uint32
I32 = jnp.int32


def _sigmoid(v):
    return 1.0 / (1.0 + jnp.exp(-v))


def _dot(a, b):
    return jnp.dot(a, b, preferred_element_type=F32)


def _dot_nt(a, b):
    return lax.dot_general(a, b, (((1,), (1,)), ((), ())), preferred_element_type=F32)


ROW_TILE = 4
LANES = 128
HALF = ROW_TILE * LANES


def _store_packed_rows(ref, val):
    m = val.shape[0]
    for i in range(ROW_TILE):
        lo = val[:, i * LANES:(i + 1) * LANES]
        hi = val[:, HALF + i * LANES:HALF + (i + 1) * LANES]
        ref[pl.ds(i, m, stride=ROW_TILE), :] = pltpu.pack_elementwise([lo, hi], packed_dtype=BF16)


def _load_packed_chunks(ref, m):
    words = [ref[pl.ds(i, m, stride=ROW_TILE), :] for i in range(ROW_TILE)]
    unpack = lambda w, idx: pltpu.unpack_elementwise(w, index=idx, packed_dtype=BF16, unpacked_dtype=F32)
    return [unpack(w, 0) for w in words] + [unpack(w, 1) for w in words]


def _layer_norm(r, g, b):
    mu = jnp.mean(r, axis=-1, keepdims=True)
    c = r - mu
    var = jnp.mean(c * c, axis=-1, keepdims=True)
    return c * lax.rsqrt(var + LN_EPS) * g + b


def _proj_kernel(x_ref, wc_ref, wqkv_ref, wg_ref, bg_ref, wconv_ref, wco_ref,
                 cos_ref, sinlo_ref, sinhi_ref,
                 a_ref, gattn_ref, q_ref, k_ref, vt_ref, kmean_ref,
                 halo_ref, *, tiles_per_seq):
    tm = x_ref.shape[0]
    i = pl.program_id(0)
    xb = x_ref[...].astype(BF16)

    cb = _dot(xb, wc_ref[:, 0:CONV_WIDTH])
    cc = _dot(xb, wc_ref[:, CONV_WIDTH:2 * CONV_WIDTH])
    cv = _dot(xb, wc_ref[:, 2 * CONV_WIDTH:3 * CONV_WIDTH])
    u = cc * cv
    @pl.when((i % tiles_per_seq) == 0)
    def _():
        halo_ref[...] = jnp.zeros(halo_ref.shape, F32)

    prev = halo_ref[...]
    row = lax.broadcasted_iota(I32, u.shape, 0)
    u1 = jnp.where(row == 0, prev[7:8, :], pltpu.roll(u, 1, 0))
    u2 = jnp.where(row == 0, prev[6:7, :], jnp.where(row == 1, prev[7:8, :], pltpu.roll(u, 2, 0)))
    halo_ref[...] = u[tm - 8:tm, :]
    w0 = wconv_ref[0:1, :]
    w1 = wconv_ref[1:2, :]
    w2 = wconv_ref[2:3, :]
    yc = cb * (w0 * u2 + w1 * u1 + w2 * u)
    y_conv = _dot(yc.astype(BF16), wco_ref[...])

    g = _dot(xb, wg_ref[...]) + bg_ref[...]
    gates = _sigmoid(g)
    a_ref[...] = (gates[:, :D_MODEL] * y_conv).astype(a_ref.dtype)
    gattn_ref[...] = gates[:, D_MODEL:].astype(gattn_ref.dtype)

    qkv = _dot(xb, wqkv_ref[...])
    cos = cos_ref[...]
    sinlo = sinlo_ref[...]
    sinhi = sinhi_ref[...]

    def rope(t):
        return (t * cos + pltpu.roll(t, ATTN_WIDTH - ROPE_DIM // 2, 1) * sinlo
                + pltpu.roll(t, ROPE_DIM // 2, 1) * sinhi)

    q = rope(qkv[:, 0:ATTN_WIDTH])
    k = rope(qkv[:, ATTN_WIDTH:2 * ATTN_WIDTH])
    q_ref[...] = (q * QK_SCALE).astype(q_ref.dtype)
    k_ref[...] = k.astype(k_ref.dtype)
    v = qkv[:, 2 * ATTN_WIDTH:]
    for blk in range(tm // MOBA_BLOCK):
        rows = slice(blk * MOBA_BLOCK, (blk + 1) * MOBA_BLOCK)
        kmean_ref[blk] = jnp.mean(k[rows, :], axis=0, keepdims=True)
        vt_ref[blk] = v[rows, :].T.astype(vt_ref.dtype)


def _proj(x2, wc, wqkv, wg, bg, wconv, wco, cos, sinlo, sinhi, seq):
    t = x2.shape[0]
    tm = PROJ_ROWS
    tiles_per_seq = seq // tm
    const = lambda i: (0, 0)
    rows = lambda i: (i, 0)
    pos = lambda i: (i % tiles_per_seq, 0)
    return pl.pallas_call(
        functools.partial(_proj_kernel, tiles_per_seq=tiles_per_seq),
        name="proj",
        grid=(t // tm,),
        in_specs=[
            pl.BlockSpec((tm, D_MODEL), rows),
            pl.BlockSpec(wc.shape, const),
            pl.BlockSpec(wqkv.shape, const),
            pl.BlockSpec(wg.shape, const),
            pl.BlockSpec(bg.shape, const),
            pl.BlockSpec(wconv.shape, const),
            pl.BlockSpec(wco.shape, const),
            pl.BlockSpec((tm, ATTN_WIDTH), pos),
            pl.BlockSpec((tm, ATTN_WIDTH), pos),
            pl.BlockSpec((tm, ATTN_WIDTH), pos),
        ],
        out_specs=[
            pl.BlockSpec((tm, D_MODEL), rows),
            pl.BlockSpec((tm, D_MODEL), rows),
            pl.BlockSpec((tm, ATTN_WIDTH), rows),
            pl.BlockSpec((tm, ATTN_WIDTH), rows),
            pl.BlockSpec((tm // MOBA_BLOCK, ATTN_WIDTH, MOBA_BLOCK), lambda i: (i, 0, 0)),
            pl.BlockSpec((tm // MOBA_BLOCK, 1, ATTN_WIDTH), lambda i: (i, 0, 0)),
        ],
        out_shape=[
            jax.ShapeDtypeStruct((t, D_MODEL), BF16),
            jax.ShapeDtypeStruct((t, D_MODEL), BF16),
            jax.ShapeDtypeStruct((t, ATTN_WIDTH), BF16),
            jax.ShapeDtypeStruct((t, ATTN_WIDTH), BF16),
            jax.ShapeDtypeStruct((t // MOBA_BLOCK, ATTN_WIDTH, MOBA_BLOCK), BF16),
            jax.ShapeDtypeStruct((t // MOBA_BLOCK, 1, ATTN_WIDTH), F32),
        ],
        scratch_shapes=[pltpu.VMEM((8, CONV_WIDTH), F32)],
        compiler_params=pltpu.CompilerParams(
            dimension_semantics=("arbitrary",), vmem_limit_bytes=VMEM_LIMIT_BYTES),
    )(x2, wc, wqkv, wg, bg, wconv, wco, cos, sinlo, sinhi)


def _attn_kernel(q_ref, k_ref, vt_ref, kmean_ref, o_ref, bias_ref, m_ref, l_ref, alpha_ref, p_ref, acc_ref,
                 *, n_blocks):
    j = pl.program_id(1)
    blk_rows = MOBA_BLOCK
    key_i = lax.broadcasted_iota(I32, (blk_rows, blk_rows), 0)
    qry_i = lax.broadcasted_iota(I32, (blk_rows, blk_rows), 1)
    causal = key_i <= qry_i
    blk_i = lax.broadcasted_iota(I32, (n_blocks, blk_rows), 0)
    past = blk_i < j

    for h in range(N_HEADS):
        cols = slice(h * HEAD_DIM, (h + 1) * HEAD_DIM)
        qh = q_ref[:, cols]
        km = kmean_ref[:, 0, cols]
        gate = _dot_nt(km, qh.astype(F32))
        gate = jnp.where(past, gate, -jnp.inf)
        rank = jnp.zeros(gate.shape, F32)
        for m in range(n_blocks):
            gm = gate[m:m + 1, :]
            ahead = jnp.where(gm > gate, 1.0, jnp.where(gm == gate, jnp.where(blk_i > m, 1.0, 0.0), 0.0))
            rank = rank + ahead
        bias_ref[h] = jnp.where(past, jnp.where(rank < MOBA_TOPK, 0.0, NEG_BIG), NEG_BIG)

    def scores(n):
        r = pl.multiple_of(n * blk_rows, blk_rows)
        return [_dot_nt(k_ref[pl.ds(r, blk_rows), h * HEAD_DIM:(h + 1) * HEAD_DIM],
                        q_ref[:, h * HEAD_DIM:(h + 1) * HEAD_DIM]) for h in range(N_HEADS)]

    for h, s_raw in enumerate(scores(j)):
        s = jnp.where(causal, s_raw, NEG_BIG)
        m0 = jnp.max(s, axis=0, keepdims=True)
        p = jnp.exp(s - m0)
        m_ref[h:h + 1, :] = m0
        l_ref[h:h + 1, :] = jnp.sum(p, axis=0, keepdims=True)
        p_ref[h] = p.astype(p_ref.dtype)
    for h in range(N_HEADS):
        cols = slice(h * HEAD_DIM, (h + 1) * HEAD_DIM)
        acc_ref[cols, :] = _dot(vt_ref[j, cols, :], p_ref[h])

    def body(n, carry):
        for h, s_raw in enumerate(scores(n)):
            sn = s_raw + bias_ref[h, pl.ds(n, 1), :]
            m_run = m_ref[h:h + 1, :]
            m_new = jnp.maximum(m_run, jnp.max(sn, axis=0, keepdims=True))
            alpha = jnp.exp(m_run - m_new)
            pn = jnp.exp(sn - m_new)
            m_ref[h:h + 1, :] = m_new
            l_ref[h:h + 1, :] = alpha * l_ref[h:h + 1, :] + jnp.sum(pn, axis=0, keepdims=True)
            alpha_ref[h:h + 1, :] = alpha
            p_ref[h] = pn.astype(p_ref.dtype)
        pvs = [_dot(vt_ref[n, h * HEAD_DIM:(h + 1) * HEAD_DIM, :], p_ref[h]) for h in range(N_HEADS)]
        for h in range(N_HEADS):
            cols = slice(h * HEAD_DIM, (h + 1) * HEAD_DIM)
            acc_ref[cols, :] = acc_ref[cols, :] * alpha_ref[h:h + 1, :] + pvs[h]
        return carry

    lax.fori_loop(0, j, body, 0)
    for h in range(N_HEADS):
        cols = slice(h * HEAD_DIM, (h + 1) * HEAD_DIM)
        acc_ref[cols, :] = acc_ref[cols, :] / l_ref[h:h + 1, :]
    o_ref[...] = acc_ref[...].T.astype(o_ref.dtype)


def _attention(q, k, vt, kmean, batch, seq):
    n_blocks = seq // MOBA_BLOCK
    return pl.pallas_call(
        functools.partial(_attn_kernel, n_blocks=n_blocks),
        name="attention",
        grid=(batch, n_blocks),
        in_specs=[
            pl.BlockSpec((MOBA_BLOCK, ATTN_WIDTH), lambda b, j: (b * n_blocks + j, 0)),
            pl.BlockSpec((seq, ATTN_WIDTH), lambda b, j: (b, 0)),
            pl.BlockSpec((n_blocks, ATTN_WIDTH, MOBA_BLOCK), lambda b, j: (b, 0, 0)),
            pl.BlockSpec((n_blocks, 1, ATTN_WIDTH), lambda b, j: (b, 0, 0)),
        ],
        out_specs=pl.BlockSpec((MOBA_BLOCK, ATTN_WIDTH), lambda b, j: (b * n_blocks + j, 0)),
        out_shape=jax.ShapeDtypeStruct(q.shape, BF16),
        scratch_shapes=[
            pltpu.VMEM((N_HEADS, n_blocks, MOBA_BLOCK), F32),
            pltpu.VMEM((N_HEADS, MOBA_BLOCK), F32),
            pltpu.VMEM((N_HEADS, MOBA_BLOCK), F32),
            pltpu.VMEM((N_HEADS, MOBA_BLOCK), F32),
            pltpu.VMEM((N_HEADS, MOBA_BLOCK, MOBA_BLOCK), BF16),
            pltpu.VMEM((ATTN_WIDTH, MOBA_BLOCK), F32),
        ],
        compiler_params=pltpu.CompilerParams(
            dimension_semantics=("arbitrary", "arbitrary"), vmem_limit_bytes=VMEM_LIMIT_BYTES),
    )(q, k, vt, kmean)


def _post_kernel(o_ref, a_ref, gattn_ref, x_ref, wao_ref, wmo_ref, g1_ref, b1_ref,
                 ws1_ref, ws3_ref, ws2_ref, wrt_ref,
                 base_ref, ht_ref, logit_ref):
    y_attn = _dot(o_ref[...], wao_ref[...])
    merged = a_ref[...].astype(F32) + gattn_ref[...].astype(F32) * y_attn
    mix = _dot(merged.astype(BF16), wmo_ref[...])
    h = _layer_norm(DN_ALPHA * x_ref[...] + mix, g1_ref[...], b1_ref[...])
    hb = h.astype(BF16)
    s1 = _dot(hb, ws1_ref[...])
    s3 = _dot(hb, ws3_ref[...])
    shared = _dot((s1 * _sigmoid(s1) * s3).astype(BF16), ws2_ref[...])
    base_ref[...] = DN_ALPHA * h + shared
    _store_packed_rows(ht_ref, h)
    logit_ref[...] = _dot_nt(wrt_ref[...], h)


def _post(o, a, gattn, x2, wao, wmo, g1, b1, ws1, ws3, ws2, wrt):
    t = x2.shape[0]
    tm = PROJ_ROWS
    const = lambda i: (0, 0)
    rows = lambda i: (i, 0)
    return pl.pallas_call(
        _post_kernel,
        name="post",
        grid=(t // tm,),
        in_specs=[
            pl.BlockSpec((tm, ATTN_WIDTH), rows),
            pl.BlockSpec((tm, D_MODEL), rows),
            pl.BlockSpec((tm, D_MODEL), rows),
            pl.BlockSpec((tm, D_MODEL), rows),
            pl.BlockSpec(wao.shape, const),
            pl.BlockSpec(wmo.shape, const),
            pl.BlockSpec(g1.shape, const),
            pl.BlockSpec(b1.shape, const),
            pl.BlockSpec(ws1.shape, const),
            pl.BlockSpec(ws3.shape, const),
            pl.BlockSpec(ws2.shape, const),
            pl.BlockSpec(wrt.shape, const),
        ],
        out_specs=[
            pl.BlockSpec((tm, D_MODEL), rows),
            pl.BlockSpec((tm * ROW_TILE, LANES), rows),
            pl.BlockSpec((N_EXPERTS, tm), lambda i: (0, i)),
        ],
        out_shape=[
            jax.ShapeDtypeStruct((t, D_MODEL), F32),
            jax.ShapeDtypeStruct((t * ROW_TILE, LANES), U32),
            jax.ShapeDtypeStruct((N_EXPERTS, t), F32),
        ],
        compiler_params=pltpu.CompilerParams(
            dimension_semantics=("arbitrary",), vmem_limit_bytes=VMEM_LIMIT_BYTES),
    )(o, a, gattn, x2, wao, wmo, g1, b1, ws1, ws3, ws2, wrt)


def _route_kernel(logit_ref, bias_ref, tri_ref, eid_ref, pos_ref, wt_ref, cnt_ref, run_ref):
    i = pl.program_id(0)
    tm = logit_ref.shape[1]
    per_group = N_EXPERTS // N_GROUPS

    @pl.when(i == 0)
    def _():
        run_ref[...] = jnp.zeros(run_ref.shape, F32)

    scores = _sigmoid(logit_ref[...])
    biased = scores + bias_ref[...]
    b3 = biased.reshape(N_GROUPS, per_group, tm)
    i3 = lax.broadcasted_iota(I32, b3.shape, 1)
    m1 = jnp.max(b3, axis=1)
    first = jnp.min(jnp.where(b3 == m1[:, None, :], i3, per_group), axis=1)
    m2 = jnp.max(jnp.where(i3 == first[:, None, :], -jnp.inf, b3), axis=1)
    gscore = m1 + m2
    g_i = lax.broadcasted_iota(I32, gscore.shape, 0)
    grank = jnp.zeros(gscore.shape, F32)
    for m in range(N_GROUPS):
        gm = gscore[m:m + 1, :]
        ahead = jnp.where(gm > gscore, 1.0, jnp.where(gm == gscore, jnp.where(g_i > m, 1.0, 0.0), 0.0))
        grank = grank + ahead
    gsel = grank < TOPK_GROUPS
    work = jnp.where(gsel[:, None, :], b3, -jnp.inf).reshape(N_EXPERTS, tm)

    e_i = lax.broadcasted_iota(I32, (N_EXPERTS, tm), 0)
    sel = jnp.zeros((N_EXPERTS, tm), F32)
    idxs = []
    scs = []
    for _k in range(TOP_K):
        mx = jnp.max(work, axis=0, keepdims=True)
        idx = jnp.min(jnp.where(work == mx, e_i, N_EXPERTS), axis=0, keepdims=True)
        hit = e_i == idx
        scs.append(jnp.sum(jnp.where(hit, scores, 0.0), axis=0, keepdims=True))
        sel = sel + jnp.where(hit, 1.0, 0.0)
        work = jnp.where(hit, -jnp.inf, work)
        idxs.append(idx)

    rank = _dot(sel.astype(BF16), tri_ref[...]) + run_ref[...]
    run_ref[...] = run_ref[...] + jnp.sum(sel, axis=1, keepdims=True)
    cnt_ref[...] = run_ref[...]

    wsum = scs[0]
    for kk in range(1, TOP_K):
        wsum = wsum + scs[kk]
    for kk in range(TOP_K):
        pos = jnp.sum(jnp.where(e_i == idxs[kk], rank, 0.0), axis=0, keepdims=True)
        eid_ref[kk:kk + 1, :] = idxs[kk]
        pos_ref[kk:kk + 1, :] = pos.astype(I32)
        wt_ref[kk:kk + 1, :] = scs[kk] / wsum * ROUTED_SCALE


def _route(logits_t, bias_col, tri):
    t = logits_t.shape[1]
    tm = ROUTE_COLS
    cols = lambda i: (0, i)
    const = lambda i: (0, 0)
    return pl.pallas_call(
        _route_kernel,
        name="route",
        grid=(t // tm,),
        in_specs=[
            pl.BlockSpec((N_EXPERTS, tm), cols),
            pl.BlockSpec((N_EXPERTS, 1), const),
            pl.BlockSpec((tm, tm), const),
        ],
        out_specs=[
            pl.BlockSpec((TOP_K, tm), cols),
            pl.BlockSpec((TOP_K, tm), cols),
            pl.BlockSpec((TOP_K, tm), cols),
            pl.BlockSpec((N_EXPERTS, 1), const),
        ],
        out_shape=[
            jax.ShapeDtypeStruct((TOP_K, t), I32),
            jax.ShapeDtypeStruct((TOP_K, t), I32),
            jax.ShapeDtypeStruct((TOP_K, t), F32),
            jax.ShapeDtypeStruct((N_EXPERTS, 1), F32),
        ],
        scratch_shapes=[pltpu.VMEM((N_EXPERTS, 1), F32)],
        compiler_params=pltpu.CompilerParams(
            dimension_semantics=("arbitrary",), vmem_limit_bytes=VMEM_LIMIT_BYTES),
    )(logits_t, bias_col, tri)


def _slots_kernel(eid_ref, pos_ref, pstart_ref, dest_ref):
    tm = eid_ref.shape[1]
    e_i = lax.broadcasted_iota(I32, (N_EXPERTS, tm), 0)
    pstart = pstart_ref[...]
    for kk in range(TOP_K):
        start = jnp.sum(jnp.where(e_i == eid_ref[kk:kk + 1, :], pstart, 0.0), axis=0, keepdims=True)
        dest_ref[kk:kk + 1, :] = start.astype(I32) + pos_ref[kk:kk + 1, :]


def _slots(eid, pos, pstart_col):
    t = eid.shape[1]
    tm = SLOT_COLS
    cols = lambda i: (0, i)
    return pl.pallas_call(
        _slots_kernel,
        name="slots",
        grid=(t // tm,),
        in_specs=[
            pl.BlockSpec((TOP_K, tm), cols),
            pl.BlockSpec((TOP_K, tm), cols),
            pl.BlockSpec((N_EXPERTS, 1), lambda i: (0, 0)),
        ],
        out_specs=pl.BlockSpec((TOP_K, tm), cols),
        out_shape=jax.ShapeDtypeStruct((TOP_K, t), I32),
        compiler_params=pltpu.CompilerParams(dimension_semantics=("arbitrary",)),
    )(eid, pos, pstart_col)


def _row_tile_at(ref, row):
    return ref.at[pl.ds(pl.multiple_of(row * ROW_TILE, ROW_TILE), ROW_TILE)]


def _dispatch_kernel(dest_ref, pad_lo_ref, pad_hi_ref, nvalid_ref, ht_ref, xpad_ref,
                     stage_ref, zero_ref, sem, pad_sem, tail_sem, *, experts_per_step):
    i = pl.program_id(0)
    n = pl.num_programs(0)
    tm = DISPATCH_ROWS
    slot = i % 2

    def pad_range(step, q):
        e = step * experts_per_step + q
        ec = jnp.minimum(e, N_EXPERTS - 1)
        lo = pad_lo_ref[ec]
        return lo, jnp.where(e < N_EXPERTS, pad_hi_ref[ec], lo)

    def start_pad(step):
        for q in range(experts_per_step):
            lo, hi = pad_range(step, q)

            def start_zero(s, c):
                pltpu.make_async_copy(
                    _row_tile_at(zero_ref, s - lo), _row_tile_at(xpad_ref, s), pad_sem).start()
                return c

            lax.fori_loop(lo, hi, start_zero, 0)

    def wait_pad(step):
        for q in range(experts_per_step):
            lo, hi = pad_range(step, q)
            count = hi - lo
            p = EXPERT_ROWS // 2
            while p >= 1:
                @pl.when((count & p) != 0)
                def _(p=p):
                    pltpu.make_async_copy(
                        zero_ref.at[pl.ds(0, p * ROW_TILE)], xpad_ref.at[pl.ds(0, p * ROW_TILE)], pad_sem).wait()
                p //= 2

    def tail_copy(tile):
        rows = EXPERT_ROWS * ROW_TILE
        return pltpu.make_async_copy(
            zero_ref, xpad_ref.at[pl.ds(pl.multiple_of(tile * rows, rows), rows)], tail_sem)

    def start_tail(tile, c):
        tail_copy(tile).start()
        return c

    def wait_tail(tile, c):
        tail_copy(tile).wait()
        return c

    n_tiles = xpad_ref.shape[0] // (EXPERT_ROWS * ROW_TILE)

    @pl.when(i == 0)
    def _():
        zero_ref[...] = jnp.zeros(zero_ref.shape, zero_ref.dtype)
        lax.fori_loop(nvalid_ref[0], n_tiles, start_tail, 0)

    start_pad(i)

    stage_ref[slot] = ht_ref[...]

    def issue(r, carry):
        src = _row_tile_at(stage_ref.at[slot], r)
        for kk in range(TOP_K):
            pltpu.make_async_copy(
                src, _row_tile_at(xpad_ref, dest_ref[r * TOP_K + kk]), sem.at[slot]).start(priority=kk % 2)
        return carry

    lax.fori_loop(0, tm, issue, 0)

    def drain(s):
        for _kk in range(TOP_K):
            pltpu.make_async_copy(
                stage_ref.at[s], xpad_ref.at[pl.ds(0, tm * ROW_TILE)], sem.at[s]).wait()

    @pl.when(i > 0)
    def _():
        drain(1 - slot)
        wait_pad(i - 1)

    @pl.when(i == n - 1)
    def _():
        drain(slot)
        wait_pad(i)
        lax.fori_loop(nvalid_ref[0], n_tiles, wait_tail, 0)


def _dispatch(dest_flat, pad_lo, pad_hi, n_valid, ht, cap):
    t = ht.shape[0] // ROW_TILE
    tm = DISPATCH_ROWS
    experts_per_step = -(-N_EXPERTS // (t // tm))
    return pl.pallas_call(
        functools.partial(_dispatch_kernel, experts_per_step=experts_per_step),
        name="dispatch",
        grid=(t // tm,),
        in_specs=[
            pl.BlockSpec((tm * TOP_K,), lambda i: (i,), memory_space=pltpu.SMEM),
            pl.BlockSpec(memory_space=pltpu.SMEM),
            pl.BlockSpec(memory_space=pltpu.SMEM),
            pl.BlockSpec(memory_space=pltpu.SMEM),
            pl.BlockSpec((tm * ROW_TILE, LANES), lambda i: (i, 0)),
        ],
        out_specs=pl.BlockSpec(memory_space=pl.ANY),
        out_shape=jax.ShapeDtypeStruct((cap * ROW_TILE, LANES), ht.dtype),
        scratch_shapes=[
            pltpu.VMEM((2, tm * ROW_TILE, LANES), ht.dtype),
            pltpu.VMEM((EXPERT_ROWS * ROW_TILE, LANES), ht.dtype),
            pltpu.SemaphoreType.DMA((2,)),
            pltpu.SemaphoreType.DMA(()),
            pltpu.SemaphoreType.DMA(()),
        ],
        compiler_params=pltpu.CompilerParams(
            dimension_semantics=("arbitrary",), has_side_effects=True),
    )(dest_flat, pad_lo, pad_hi, n_valid, ht)


def _expert_kernel(texp_ref, nvalid_ref, first_ref, next_ref, par_ref,
                   x_hbm, w1_hbm, w3_hbm, w2_hbm, y_ref,
                   w1f_ref, w3f_ref, w2f_ref, w1b_ref, w3b_ref, w2b_ref, hid_ref, xbuf_ref, sem, xsem):
    i = pl.program_id(0)
    n_tiles = pl.num_programs(0) - 1
    tm = EXPERT_ROWS
    cur = jnp.minimum(i, n_tiles - 1)
    prev = jnp.maximum(i - 1, 0)
    cur_valid = i < nvalid_ref[0]
    prev_valid = jnp.logical_and(i >= 1, i - 1 < nvalid_ref[0])

    def x_copy(tile):
        rows = tm * ROW_TILE
        return pltpu.make_async_copy(
            x_hbm.at[pl.ds(pl.multiple_of(tile * rows, rows), rows)],
            xbuf_ref.at[tile % X_BUFFERS], xsem.at[tile % X_BUFFERS])

    @pl.when(i == 0)
    def _():
        for t in range(X_BUFFERS - 1):
            @pl.when(t < nvalid_ref[0])
            def _(t=t):
                x_copy(t).start()

    @pl.when(i + (X_BUFFERS - 1) < nvalid_ref[0])
    def _():
        x_copy(i + (X_BUFFERS - 1)).start()

    @pl.when(cur_valid)
    def _():
        x_copy(i).wait()

    def weight_copies(e, s):
        return (pltpu.make_async_copy(w1_hbm.at[e], w1f_ref.at[s], sem.at[s]),
                pltpu.make_async_copy(w3_hbm.at[e], w3f_ref.at[s], sem.at[s]),
                pltpu.make_async_copy(w2_hbm.at[e], w2f_ref.at[s], sem.at[s]))

    @pl.when(jnp.logical_and(cur_valid, first_ref[cur] == 1))
    def _():
        s = par_ref[cur]

        @pl.when(i == 0)
        def _():
            for c in weight_copies(texp_ref[0], 0):
                c.start()

        for c in weight_copies(texp_ref[cur], s):
            c.wait()
        nxt = next_ref[cur]

        @pl.when(nxt >= 0)
        def _():
            for c in weight_copies(nxt, 1 - s):
                c.start()

        w1b_ref[...] = w1f_ref[s].astype(BF16)
        w3b_ref[...] = w3f_ref[s].astype(BF16)
        w2b_ref[s] = w2f_ref[s].astype(BF16)

    def up_matmuls():
        chunks = _load_packed_chunks(xbuf_ref.at[i % X_BUFFERS], tm)
        xb = jnp.concatenate([c.astype(BF16) for c in chunks], axis=1)
        return _dot(xb, w1b_ref[...]), _dot(xb, w3b_ref[...])

    def down_matmul():
        return _dot(hid_ref[(i - 1) % 2], w2b_ref[par_ref[prev]])

    def store_hidden(g, u):
        hid_ref[i % 2] = (g * _sigmoid(g) * u).astype(BF16)

    @pl.when(jnp.logical_and(cur_valid, prev_valid))
    def _():
        g, u = up_matmuls()
        y = down_matmul()
        store_hidden(g, u)
        _store_packed_rows(y_ref, y)

    @pl.when(jnp.logical_and(cur_valid, jnp.logical_not(prev_valid)))
    def _():
        store_hidden(*up_matmuls())

    @pl.when(jnp.logical_and(jnp.logical_not(cur_valid), prev_valid))
    def _():
        _store_packed_rows(y_ref, down_matmul())

    @pl.when(jnp.logical_and(i >= 1, jnp.logical_not(prev_valid)))
    def _():
        y_ref[...] = jnp.zeros(y_ref.shape, y_ref.dtype)


def _experts(tile_expert, n_valid, tile_first, tile_next, tile_par, xpad, w1, w3, w2):
    tm = EXPERT_ROWS
    n_tiles = xpad.shape[0] // (tm * ROW_TILE)

    grid_spec = pltpu.PrefetchScalarGridSpec(
        num_scalar_prefetch=5,
        grid=(n_tiles + 1,),
        in_specs=[
            pl.BlockSpec(memory_space=pl.ANY),
            pl.BlockSpec(memory_space=pl.ANY),
            pl.BlockSpec(memory_space=pl.ANY),
            pl.BlockSpec(memory_space=pl.ANY),
        ],
        out_specs=pl.BlockSpec((tm * ROW_TILE, LANES), lambda i, te, nv, fi, nx, pa: (jnp.maximum(i - 1, 0), 0)),
        scratch_shapes=[
            pltpu.VMEM((2, D_MODEL, EXPERT_DIM), F32),
            pltpu.VMEM((2, D_MODEL, EXPERT_DIM), F32),
            pltpu.VMEM((2, EXPERT_DIM, D_MODEL), F32),
            pltpu.VMEM((D_MODEL, EXPERT_DIM), BF16),
            pltpu.VMEM((D_MODEL, EXPERT_DIM), BF16),
            pltpu.VMEM((2, EXPERT_DIM, D_MODEL), BF16),
            pltpu.VMEM((2, tm, EXPERT_DIM), BF16),
            pltpu.VMEM((X_BUFFERS, tm * ROW_TILE, LANES), xpad.dtype),
            pltpu.SemaphoreType.DMA((2,)),
            pltpu.SemaphoreType.DMA((X_BUFFERS,)),
        ],
    )
    return pl.pallas_call(
        _expert_kernel,
        name="experts",
        grid_spec=grid_spec,
        out_shape=jax.ShapeDtypeStruct(xpad.shape, xpad.dtype),
        compiler_params=pltpu.CompilerParams(
            dimension_semantics=("arbitrary",), vmem_limit_bytes=VMEM_LIMIT_BYTES),
    )(tile_expert, n_valid, tile_first, tile_next, tile_par, xpad, w1, w3, w2)


def _combine_kernel(dest_ref, dest_next_ref, wt_ref, base_ref, g2_ref, b2_ref, ypad_ref, out_ref, buf_ref, sem):
    i = pl.program_id(0)
    n = pl.num_programs(0)
    tm = COMBINE_ROWS
    slot = i % 2

    def issue_from(d_ref, s):
        def issue(r, carry):
            for kk in range(TOP_K):
                pltpu.make_async_copy(
                    _row_tile_at(ypad_ref, d_ref[r * TOP_K + kk]), _row_tile_at(buf_ref.at[s, kk], r),
                    sem.at[s]).start(priority=kk % 2)
            return carry
        lax.fori_loop(0, tm, issue, 0)

    @pl.when(i == 0)
    def _():
        issue_from(dest_ref, 0)

    @pl.when(i + 1 < n)
    def _():
        issue_from(dest_next_ref, 1 - slot)

    for kk in range(TOP_K):
        pltpu.make_async_copy(
            ypad_ref.at[pl.ds(0, tm * ROW_TILE)], buf_ref.at[slot, kk], sem.at[slot]).wait()

    wt = wt_ref[...]
    n_chunks = D_MODEL // LANES
    parts = [base_ref[:, c * LANES:(c + 1) * LANES] for c in range(n_chunks)]
    for kk in range(TOP_K):
        wk = wt[:, kk:kk + 1]
        chunks = _load_packed_chunks(buf_ref.at[slot, kk], tm)
        parts = [acc + wk * c for acc, c in zip(parts, chunks)]
    r = jnp.concatenate(parts, axis=1)
    out_ref[...] = _layer_norm(r, g2_ref[...], b2_ref[...])


def _combine(dest_flat, wt_tok, base, g2, b2, ypad):
    t = base.shape[0]
    tm = COMBINE_ROWS
    n = t // tm
    rows = lambda i: (i, 0)
    const = lambda i: (0, 0)
    return pl.pallas_call(
        _combine_kernel,
        name="combine",
        grid=(n,),
        in_specs=[
            pl.BlockSpec((tm * TOP_K,), lambda i: (i,), memory_space=pltpu.SMEM),
            pl.BlockSpec((tm * TOP_K,), lambda i: (jnp.minimum(i + 1, n - 1),), memory_space=pltpu.SMEM),
            pl.BlockSpec((tm, TOP_K), rows),
            pl.BlockSpec((tm, D_MODEL), rows),
            pl.BlockSpec(g2.shape, const),
            pl.BlockSpec(b2.shape, const),
            pl.BlockSpec(memory_space=pl.ANY),
        ],
        out_specs=pl.BlockSpec((tm, D_MODEL), rows),
        out_shape=jax.ShapeDtypeStruct((t, D_MODEL), F32),
        scratch_shapes=[
            pltpu.VMEM((2, TOP_K, tm * ROW_TILE, LANES), ypad.dtype),
            pltpu.SemaphoreType.DMA((2,)),
        ],
        compiler_params=pltpu.CompilerParams(
            dimension_semantics=("arbitrary",), vmem_limit_bytes=VMEM_LIMIT_BYTES),
    )(dest_flat, dest_flat, wt_tok, base, g2, b2, ypad)


def _rope_tables(seq):
    half = ROPE_DIM // 2
    inv = jnp.power(ROPE_THETA, -jnp.arange(half, dtype=F32) * 2.0 / ROPE_DIM)
    ang = jnp.arange(seq, dtype=F32)[:, None] * inv[None, :]
    cos = jnp.cos(ang)
    sin = jnp.sin(ang)
    ones = jnp.ones((seq, HEAD_DIM - ROPE_DIM), F32)
    zeros_rest = jnp.zeros((seq, HEAD_DIM - ROPE_DIM), F32)
    zeros_half = jnp.zeros((seq, half), F32)
    cos_h = jnp.concatenate([cos, cos, ones], axis=1)
    sinlo_h = jnp.concatenate([-sin, zeros_half, zeros_rest], axis=1)
    sinhi_h = jnp.concatenate([zeros_half, sin, zeros_rest], axis=1)
    tile = lambda m: jnp.tile(m, (1, N_HEADS))
    return tile(cos_h), tile(sinlo_h), tile(sinhi_h)


def kernel(x, w_in, b_gate, w_conv, w_conv_out, w_attn_out, w_mix_out, ln1_g, ln1_b, w_router, router_bias,
           w1, w3, w2, ws1, ws3, ws2, ln2_g, ln2_b):
    batch, seq, d = x.shape
    t = batch * seq
    assert d == D_MODEL and seq % PROJ_ROWS == 0 and seq % MOBA_BLOCK == 0
    x2 = x.reshape(t, d)
    h = None
    for layer in range(DEPTH):
        c3 = 3 * CONV_WIDTH
        a3 = 3 * ATTN_WIDTH
        w_in_b = w_in[layer].astype(BF16)
        wc = w_in_b[:, :c3]
        wqkv = w_in_b[:, c3:c3 + a3]
        wg = w_in_b[:, c3 + a3:]
        cos, sinlo, sinhi = _rope_tables(seq)
        a, gattn, q, k, v, kmean = _proj(
            x2, wc, wqkv, wg, b_gate[layer][None, :], w_conv[layer], w_conv_out[layer].astype(BF16),
            cos, sinlo, sinhi, seq)
        o = _attention(q, k, v, kmean, batch, seq)
        base, ht, logits_t = _post(
            o, a, gattn, x2, w_attn_out[layer].astype(BF16), w_mix_out[layer].astype(BF16),
            ln1_g[layer][None, :], ln1_b[layer][None, :],
            ws1[layer].astype(BF16), ws3[layer].astype(BF16), ws2[layer].astype(BF16),
            w_router[layer].T)

        tri = (jnp.arange(ROUTE_COLS)[:, None] < jnp.arange(ROUTE_COLS)[None, :]).astype(BF16)
        eid, pos, wts, counts = _route(logits_t, router_bias[layer][:, None], tri)

        cnt = counts[:, 0].astype(I32)
        padded = (cnt + EXPERT_ROWS - 1) // EXPERT_ROWS * EXPERT_ROWS
        pend = jnp.cumsum(padded)
        pstart = pend - padded
        cap = t * TOP_K + N_EXPERTS * EXPERT_ROWS
        n_tiles = cap // EXPERT_ROWS
        dest = _slots(eid, pos, pstart.astype(F32)[:, None])
        dest_flat = dest.T.reshape(t * TOP_K)
        tile_start = jnp.arange(n_tiles, dtype=I32) * EXPERT_ROWS
        tile_expert = jnp.minimum(
            jnp.sum((pend[None, :] <= tile_start[:, None]).astype(I32), axis=1), N_EXPERTS - 1)
        n_valid = (pend[-1:] // EXPERT_ROWS).astype(I32)
        tile_i = jnp.arange(n_tiles, dtype=I32)
        prev_expert = jnp.concatenate([jnp.full((1,), -1, I32), tile_expert[:-1]])
        tile_first = ((tile_i < n_valid[0]) & (tile_expert != prev_expert)).astype(I32)
        first_pos = jnp.where(tile_first == 1, tile_i, n_tiles)
        later_first = lax.cummin(jnp.concatenate([first_pos[1:], jnp.full((1,), n_tiles, I32)]), reverse=True)
        tile_next = jnp.where(later_first < n_tiles, tile_expert[jnp.minimum(later_first, n_tiles - 1)], -1)
        tile_par = (jnp.cumsum(tile_first) - 1) % 2

        xpad = _dispatch(dest_flat, pstart + cnt, pend, n_valid, ht, cap)
        ypad = _experts(tile_expert, n_valid, tile_first, tile_next.astype(I32), tile_par.astype(I32),
                        xpad, w1[layer], w3[layer], w2[layer])
        h = _combine(dest_flat, wts.T, base, ln2_g[layer][None, :], ln2_b[layer][None, :], ypad)
        x2 = h
    return h.reshape(batch, seq, d)
```

```python
import functools

import jax
import jax.numpy as jnp
from jax import lax
from jax.experimental import pallas as pl
from jax.experimental.pallas import tpu as pltpu

D_MODEL = 1024
CONV_WIDTH = 1024
CONV_K = 3
N_HEADS = 8
HEAD_DIM = 64
ATTN_WIDTH = N_HEADS * HEAD_DIM
MOBA_BLOCK = 256
MOBA_TOPK = 3
ROPE_THETA = 500000.0
ROPE_DIM = HEAD_DIM // 4
N_EXPERTS = 256
TOP_K = 8
N_GROUPS = 8
TOPK_GROUPS = 4
EXPERT_DIM = 256
SHARED_DIM = 256
ROUTED_SCALE = 2.5
DEPTH = 1
DN_ALPHA = (2 * DEPTH) ** 0.25
LN_EPS = 1e-5

NEG_BIG = -1e30
QK_SCALE = HEAD_DIM ** -0.5
assert QK_SCALE == 2.0 ** -3

VMEM_LIMIT_BYTES = 56 * 1024 * 1024

PROJ_ROWS = 512
ROUTE_COLS = 256
SLOT_COLS = 1024
EXPERT_ROWS = 256
X_BUFFERS = 4
DISPATCH_ROWS = 256
COMBINE_ROWS = 256

F32 = jnp.float32
BF16 = jnp.bfloat16
U32 = jnp.uint32
I32 = jnp.int32


def _sigmoid(v):
    return 1.0 / (1.0 + jnp.exp(-v))


def _dot(a, b):
    return jnp.dot(a, b, preferred_element_type=F32)


def _dot_nt(a, b):
    return lax.dot_general(a, b, (((1,), (1,)), ((), ())), preferred_element_type=F32)


ROW_TILE = 4
LANES = 128
HALF = ROW_TILE * LANES


def _store_packed_rows(ref, val):
    m = val.shape[0]
    for i in range(ROW_TILE):
        lo = val[:, i * LANES:(i + 1) * LANES]
        hi = val[:, HALF + i * LANES:HALF + (i + 1) * LANES]
        ref[pl.ds(i, m, stride=ROW_TILE), :] = pltpu.pack_elementwise([lo, hi], packed_dtype=BF16)


def _load_packed_chunks(ref, m):
    words = [ref[pl.ds(i, m, stride=ROW_TILE), :] for i in range(ROW_TILE)]
    unpack = lambda w, idx: pltpu.unpack_elementwise(w, index=idx, packed_dtype=BF16, unpacked_dtype=F32)
    return [unpack(w, 0) for w in words] + [unpack(w, 1) for w in words]


def _layer_norm(r, g, b):
    mu = jnp.mean(r, axis=-1, keepdims=True)
    c = r - mu
    var = jnp.mean(c * c, axis=-1, keepdims=True)
    return c * lax.rsqrt(var + LN_EPS) * g + b


def _proj_kernel(x_ref, wc_ref, wqkv_ref, wg_ref, bg_ref, wconv_ref, wco_ref,
                 cos_ref, sinlo_ref, sinhi_ref,
                 a_ref, gattn_ref, q_ref, k_ref, vt_ref, kmean_ref,
                 halo_ref, *, tiles_per_seq):
    tm = x_ref.shape[0]
    i = pl.program_id(0)
    xb = x_ref[...].astype(BF16)

    cb = _dot(xb, wc_ref[:, 0:CONV_WIDTH])
    cc = _dot(xb, wc_ref[:, CONV_WIDTH:2 * CONV_WIDTH])
    cv = _dot(xb, wc_ref[:, 2 * CONV_WIDTH:3 * CONV_WIDTH])
    u = cc * cv
    @pl.when((i % tiles_per_seq) == 0)
    def _():
        halo_ref[...] = jnp.zeros(halo_ref.shape, F32)

    prev = halo_ref[...]
    row = lax.broadcasted_iota(I32, u.shape, 0)
    u1 = jnp.where(row == 0, prev[7:8, :], pltpu.roll(u, 1, 0))
    u2 = jnp.where(row == 0, prev[6:7, :], jnp.where(row == 1, prev[7:8, :], pltpu.roll(u, 2, 0)))
    halo_ref[...] = u[tm - 8:tm, :]
    w0 = wconv_ref[0:1, :]
    w1 = wconv_ref[1:2, :]
    w2 = wconv_ref[2:3, :]
    yc = cb * (w0 * u2 + w1 * u1 + w2 * u)
    y_conv = _dot(yc.astype(BF16), wco_ref[...])

    g = _dot(xb, wg_ref[...]) + bg_ref[...]
    gates = _sigmoid(g)
    a_ref[...] = (gates[:, :D_MODEL] * y_conv).astype(a_ref.dtype)
    gattn_ref[...] = gates[:, D_MODEL:].astype(gattn_ref.dtype)

    qkv = _dot(xb, wqkv_ref[...])
    cos = cos_ref[...]
    sinlo = sinlo_ref[...]
    sinhi = sinhi_ref[...]

    def rope(t):
        return (t * cos + pltpu.roll(t, ATTN_WIDTH - ROPE_DIM // 2, 1) * sinlo
                + pltpu.roll(t, ROPE_DIM // 2, 1) * sinhi)

    q = rope(qkv[:, 0:ATTN_WIDTH])
    k = rope(qkv[:, ATTN_WIDTH:2 * ATTN_WIDTH])
    q_ref[...] = (q * QK_SCALE).astype(q_ref.dtype)
    k_ref[...] = k.astype(k_ref.dtype)
    v = qkv[:, 2 * ATTN_WIDTH:]
    for blk in range(tm // MOBA_BLOCK):
        rows = slice(blk * MOBA_BLOCK, (blk + 1) * MOBA_BLOCK)
        kmean_ref[blk] = jnp.mean(k[rows, :], axis=0, keepdims=True)
        vt_ref[blk] = v[rows, :].T.astype(vt_ref.dtype)


def _proj(x2, wc, wqkv, wg, bg, wconv, wco, cos, sinlo, sinhi, seq):
    t = x2.shape[0]
    tm = PROJ_ROWS
    tiles_per_seq = seq // tm
    const = lambda i: (0, 0)
    rows = lambda i: (i, 0)
    pos = lambda i: (i % tiles_per_seq, 0)
    return pl.pallas_call(
        functools.partial(_proj_kernel, tiles_per_seq=tiles_per_seq),
        name="proj",
        grid=(t // tm,),
        in_specs=[
            pl.BlockSpec((tm, D_MODEL), rows),
            pl.BlockSpec(wc.shape, const),
            pl.BlockSpec(wqkv.shape, const),
            pl.BlockSpec(wg.shape, const),
            pl.BlockSpec(bg.shape, const),
            pl.BlockSpec(wconv.shape, const),
            pl.BlockSpec(wco.shape, const),
            pl.BlockSpec((tm, ATTN_WIDTH), pos),
            pl.BlockSpec((tm, ATTN_WIDTH), pos),
            pl.BlockSpec((tm, ATTN_WIDTH), pos),
        ],
        out_specs=[
            pl.BlockSpec((tm, D_MODEL), rows),
            pl.BlockSpec((tm, D_MODEL), rows),
            pl.BlockSpec((tm, ATTN_WIDTH), rows),
            pl.BlockSpec((tm, ATTN_WIDTH), rows),
            pl.BlockSpec((tm // MOBA_BLOCK, ATTN_WIDTH, MOBA_BLOCK), lambda i: (i, 0, 0)),
            pl.BlockSpec((tm // MOBA_BLOCK, 1, ATTN_WIDTH), lambda i: (i, 0, 0)),
        ],
        out_shape=[
            jax.ShapeDtypeStruct((t, D_MODEL), BF16),
            jax.ShapeDtypeStruct((t, D_MODEL), BF16),
            jax.ShapeDtypeStruct((t, ATTN_WIDTH), BF16),
            jax.ShapeDtypeStruct((t, ATTN_WIDTH), BF16),
            jax.ShapeDtypeStruct((t // MOBA_BLOCK, ATTN_WIDTH, MOBA_BLOCK), BF16),
            jax.ShapeDtypeStruct((t // MOBA_BLOCK, 1, ATTN_WIDTH), F32),
        ],
        scratch_shapes=[pltpu.VMEM((8, CONV_WIDTH), F32)],
        compiler_params=pltpu.CompilerParams(
            dimension_semantics=("arbitrary",), vmem_limit_bytes=VMEM_LIMIT_BYTES),
    )(x2, wc, wqkv, wg, bg, wconv, wco, cos, sinlo, sinhi)


def _attn_kernel(q_ref, k_ref, vt_ref, kmean_ref, o_ref, bias_ref, m_ref, l_ref, alpha_ref, p_ref, acc_ref,
                 *, n_blocks):
    j = pl.program_id(1)
    blk_rows = MOBA_BLOCK
    key_i = lax.broadcasted_iota(I32, (blk_rows, blk_rows), 0)
    qry_i = lax.broadcasted_iota(I32, (blk_rows, blk_rows), 1)
    causal = key_i <= qry_i
    blk_i = lax.broadcasted_iota(I32, (n_blocks, blk_rows), 0)
    past = blk_i < j

    for h in range(N_HEADS):
        cols = slice(h * HEAD_DIM, (h + 1) * HEAD_DIM)
        qh = q_ref[:, cols]
        km = kmean_ref[:, 0, cols]
        gate = _dot_nt(km, qh.astype(F32))
        gate = jnp.where(past, gate, -jnp.inf)
        rank = jnp.zeros(gate.shape, F32)
        for m in range(n_blocks):
            gm = gate[m:m + 1, :]
            ahead = jnp.where(gm > gate, 1.0, jnp.where(gm == gate, jnp.where(blk_i > m, 1.0, 0.0), 0.0))
            rank = rank + ahead
        bias_ref[h] = jnp.where(past, jnp.where(rank < MOBA_TOPK, 0.0, NEG_BIG), NEG_BIG)

    def scores(n):
        r = pl.multiple_of(n * blk_rows, blk_rows)
        return [_dot_nt(k_ref[pl.ds(r, blk_rows), h * HEAD_DIM:(h + 1) * HEAD_DIM],
                        q_ref[:, h * HEAD_DIM:(h + 1) * HEAD_DIM]) for h in range(N_HEADS)]

    for h, s_raw in enumerate(scores(j)):
        s = jnp.where(causal, s_raw, NEG_BIG)
        m0 = jnp.max(s, axis=0, keepdims=True)
        p = jnp.exp(s - m0)
        m_ref[h:h + 1, :] = m0
        l_ref[h:h + 1, :] = jnp.sum(p, axis=0, keepdims=True)
        p_ref[h] = p.astype(p_ref.dtype)
    for h in range(N_HEADS):
        cols = slice(h * HEAD_DIM, (h + 1) * HEAD_DIM)
        acc_ref[cols, :] = _dot(vt_ref[j, cols, :], p_ref[h])

    def body(n, carry):
        for h, s_raw in enumerate(scores(n)):
            sn = s_raw + bias_ref[h, pl.ds(n, 1), :]
            m_run = m_ref[h:h + 1, :]
            m_new = jnp.maximum(m_run, jnp.max(sn, axis=0, keepdims=True))
            alpha = jnp.exp(m_run - m_new)
            pn = jnp.exp(sn - m_new)
            m_ref[h:h + 1, :] = m_new
            l_ref[h:h + 1, :] = alpha * l_ref[h:h + 1, :] + jnp.sum(pn, axis=0, keepdims=True)
            alpha_ref[h:h + 1, :] = alpha
            p_ref[h] = pn.astype(p_ref.dtype)
        pvs = [_dot(vt_ref[n, h * HEAD_DIM:(h + 1) * HEAD_DIM, :], p_ref[h]) for h in range(N_HEADS)]
        for h in range(N_HEADS):
            cols = slice(h * HEAD_DIM, (h + 1) * HEAD_DIM)
            acc_ref[cols, :] = acc_ref[cols, :] * alpha_ref[h:h + 1, :] + pvs[h]
        return carry

    lax.fori_loop(0, j, body, 0)
    for h in range(N_HEADS):
        cols = slice(h * HEAD_DIM, (h + 1) * HEAD_DIM)
        acc_ref[cols, :] = acc_ref[cols, :] / l_ref[h:h + 1, :]
    o_ref[...] = acc_ref[...].T.astype(o_ref.dtype)


def _attention(q, k, vt, kmean, batch, seq):
    n_blocks = seq // MOBA_BLOCK
    return pl.pallas_call(
        functools.partial(_attn_kernel, n_blocks=n_blocks),
        name="attention",
        grid=(batch, n_blocks),
        in_specs=[
            pl.BlockSpec((MOBA_BLOCK, ATTN_WIDTH), lambda b, j: (b * n_blocks + j, 0)),
            pl.BlockSpec((seq, ATTN_WIDTH), lambda b, j: (b, 0)),
            pl.BlockSpec((n_blocks, ATTN_WIDTH, MOBA_BLOCK), lambda b, j: (b, 0, 0)),
            pl.BlockSpec((n_blocks, 1, ATTN_WIDTH), lambda b, j: (b, 0, 0)),
        ],
        out_specs=pl.BlockSpec((MOBA_BLOCK, ATTN_WIDTH), lambda b, j: (b * n_blocks + j, 0)),
        out_shape=jax.ShapeDtypeStruct(q.shape, BF16),
        scratch_shapes=[
            pltpu.VMEM((N_HEADS, n_blocks, MOBA_BLOCK), F32),
            pltpu.VMEM((N_HEADS, MOBA_BLOCK), F32),
            pltpu.VMEM((N_HEADS, MOBA_BLOCK), F32),
            pltpu.VMEM((N_HEADS, MOBA_BLOCK), F32),
            pltpu.VMEM((N_HEADS, MOBA_BLOCK, MOBA_BLOCK), BF16),
            pltpu.VMEM((ATTN_WIDTH, MOBA_BLOCK), F32),
        ],
        compiler_params=pltpu.CompilerParams(
            dimension_semantics=("arbitrary", "arbitrary"), vmem_limit_bytes=VMEM_LIMIT_BYTES),
    )(q, k, vt, kmean)


def _post_kernel(o_ref, a_ref, gattn_ref, x_ref, wao_ref, wmo_ref, g1_ref, b1_ref,
                 ws1_ref, ws3_ref, ws2_ref, wrt_ref,
                 base_ref, ht_ref, logit_ref):
    y_attn = _dot(o_ref[...], wao_ref[...])
    merged = a_ref[...].astype(F32) + gattn_ref[...].astype(F32) * y_attn
    mix = _dot(merged.astype(BF16), wmo_ref[...])
    h = _layer_norm(DN_ALPHA * x_ref[...] + mix, g1_ref[...], b1_ref[...])
    hb = h.astype(BF16)
    s1 = _dot(hb, ws1_ref[...])
    s3 = _dot(hb, ws3_ref[...])
    shared = _dot((s1 * _sigmoid(s1) * s3).astype(BF16), ws2_ref[...])
    base_ref[...] = DN_ALPHA * h + shared
    _store_packed_rows(ht_ref, h)
    logit_ref[...] = _dot_nt(wrt_ref[...], h)


def _post(o, a, gattn, x2, wao, wmo, g1, b1, ws1, ws3, ws2, wrt):
    t = x2.shape[0]
    tm = PROJ_ROWS
    const = lambda i: (0, 0)
    rows = lambda i: (i, 0)
    return pl.pallas_call(
        _post_kernel,
        name="post",
        grid=(t // tm,),
        in_specs=[
            pl.BlockSpec((tm, ATTN_WIDTH), rows),
            pl.BlockSpec((tm, D_MODEL), rows),
            pl.BlockSpec((tm, D_MODEL), rows),
            pl.BlockSpec((tm, D_MODEL), rows),
            pl.BlockSpec(wao.shape, const),
            pl.BlockSpec(wmo.shape, const),
            pl.BlockSpec(g1.shape, const),
            pl.BlockSpec(b1.shape, const),
            pl.BlockSpec(ws1.shape, const),
            pl.BlockSpec(ws3.shape, const),
            pl.BlockSpec(ws2.shape, const),
            pl.BlockSpec(wrt.shape, const),
        ],
        out_specs=[
            pl.BlockSpec((tm, D_MODEL), rows),
            pl.BlockSpec((tm * ROW_TILE, LANES), rows),
            pl.BlockSpec((N_EXPERTS, tm), lambda i: (0, i)),
        ],
        out_shape=[
            jax.ShapeDtypeStruct((t, D_MODEL), F32),
            jax.ShapeDtypeStruct((t * ROW_TILE, LANES), U32),
            jax.ShapeDtypeStruct((N_EXPERTS, t), F32),
        ],
        compiler_params=pltpu.CompilerParams(
            dimension_semantics=("arbitrary",), vmem_limit_bytes=VMEM_LIMIT_BYTES),
    )(o, a, gattn, x2, wao, wmo, g1, b1, ws1, ws3, ws2, wrt)


def _route_kernel(logit_ref, bias_ref, tri_ref, eid_ref, pos_ref, wt_ref, cnt_ref, run_ref):
    i = pl.program_id(0)
    tm = logit_ref.shape[1]
    per_group = N_EXPERTS // N_GROUPS

    @pl.when(i == 0)
    def _():
        run_ref[...] = jnp.zeros(run_ref.shape, F32)

    scores = _sigmoid(logit_ref[...])
    biased = scores + bias_ref[...]
    b3 = biased.reshape(N_GROUPS, per_group, tm)
    i3 = lax.broadcasted_iota(I32, b3.shape, 1)
    m1 = jnp.max(b3, axis=1)
    first = jnp.min(jnp.where(b3 == m1[:, None, :], i3, per_group), axis=1)
    m2 = jnp.max(jnp.where(i3 == first[:, None, :], -jnp.inf, b3), axis=1)
    gscore = m1 + m2
    g_i = lax.broadcasted_iota(I32, gscore.shape, 0)
    grank = jnp.zeros(gscore.shape, F32)
    for m in range(N_GROUPS):
        gm = gscore[m:m + 1, :]
        ahead = jnp.where(gm > gscore, 1.0, jnp.where(gm == gscore, jnp.where(g_i > m, 1.0, 0.0), 0.0))
        grank = grank + ahead
    gsel = grank < TOPK_GROUPS
    work = jnp.where(gsel[:, None, :], b3, -jnp.inf).reshape(N_EXPERTS, tm)

    e_i = lax.broadcasted_iota(I32, (N_EXPERTS, tm), 0)
    sel = jnp.zeros((N_EXPERTS, tm), F32)
    idxs = []
    scs = []
    for _k in range(TOP_K):
        mx = jnp.max(work, axis=0, keepdims=True)
        idx = jnp.min(jnp.where(work == mx, e_i, N_EXPERTS), axis=0, keepdims=True)
        hit = e_i == idx
        scs.append(jnp.sum(jnp.where(hit, scores, 0.0), axis=0, keepdims=True))
        sel = sel + jnp.where(hit, 1.0, 0.0)
        work = jnp.where(hit, -jnp.inf, work)
        idxs.append(idx)

    rank = _dot(sel.astype(BF16), tri_ref[...]) + run_ref[...]
    run_ref[...] = run_ref[...] + jnp.sum(sel, axis=1, keepdims=True)
    cnt_ref[...] = run_ref[...]

    wsum = scs[0]
    for kk in range(1, TOP_K):
        wsum = wsum + scs[kk]
    for kk in range(TOP_K):
        pos = jnp.sum(jnp.where(e_i == idxs[kk], rank, 0.0), axis=0, keepdims=True)
        eid_ref[kk:kk + 1, :] = idxs[kk]
        pos_ref[kk:kk + 1, :] = pos.astype(I32)
        wt_ref[kk:kk + 1, :] = scs[kk] / wsum * ROUTED_SCALE


def _route(logits_t, bias_col, tri):
    t = logits_t.shape[1]
    tm = ROUTE_COLS
    cols = lambda i: (0, i)
    const = lambda i: (0, 0)
    return pl.pallas_call(
        _route_kernel,
        name="route",
        grid=(t // tm,),
        in_specs=[
            pl.BlockSpec((N_EXPERTS, tm), cols),
            pl.BlockSpec((N_EXPERTS, 1), const),
            pl.BlockSpec((tm, tm), const),
        ],
        out_specs=[
            pl.BlockSpec((TOP_K, tm), cols),
            pl.BlockSpec((TOP_K, tm), cols),
            pl.BlockSpec((TOP_K, tm), cols),
            pl.BlockSpec((N_EXPERTS, 1), const),
        ],
        out_shape=[
            jax.ShapeDtypeStruct((TOP_K, t), I32),
            jax.ShapeDtypeStruct((TOP_K, t), I32),
            jax.ShapeDtypeStruct((TOP_K, t), F32),
            jax.ShapeDtypeStruct((N_EXPERTS, 1), F32),
        ],
        scratch_shapes=[pltpu.VMEM((N_EXPERTS, 1), F32)],
        compiler_params=pltpu.CompilerParams(
            dimension_semantics=("arbitrary",), vmem_limit_bytes=VMEM_LIMIT_BYTES),
    )(logits_t, bias_col, tri)


def _slots_kernel(eid_ref, pos_ref, pstart_ref, dest_ref):
    tm = eid_ref.shape[1]
    e_i = lax.broadcasted_iota(I32, (N_EXPERTS, tm), 0)
    pstart = pstart_ref[...]
    for kk in range(TOP_K):
        start = jnp.sum(jnp.where(e_i == eid_ref[kk:kk + 1, :], pstart, 0.0), axis=0, keepdims=True)
        dest_ref[kk:kk + 1, :] = start.astype(I32) + pos_ref[kk:kk + 1, :]


def _slots(eid, pos, pstart_col):
    t = eid.shape[1]
    tm = SLOT_COLS
    cols = lambda i: (0, i)
    return pl.pallas_call(
        _slots_kernel,
        name="slots",
        grid=(t // tm,),
        in_specs=[
            pl.BlockSpec((TOP_K, tm), cols),
            pl.BlockSpec((TOP_K, tm), cols),
            pl.BlockSpec((N_EXPERTS, 1), lambda i: (0, 0)),
        ],
        out_specs=pl.BlockSpec((TOP_K, tm), cols),
        out_shape=jax.ShapeDtypeStruct((TOP_K, t), I32),
        compiler_params=pltpu.CompilerParams(dimension_semantics=("arbitrary",)),
    )(eid, pos, pstart_col)


def _row_tile_at(ref, row):
    return ref.at[pl.ds(pl.multiple_of(row * ROW_TILE, ROW_TILE), ROW_TILE)]


def _dispatch_kernel(dest_ref, pad_lo_ref, pad_hi_ref, nvalid_ref, ht_ref, xpad_ref,
                     stage_ref, zero_ref, sem, pad_sem, tail_sem, *, experts_per_step):
    i = pl.program_id(0)
    n = pl.num_programs(0)
    tm = DISPATCH_ROWS
    slot = i % 2

    def pad_range(step, q):
        e = step * experts_per_step + q
        ec = jnp.minimum(e, N_EXPERTS - 1)
        lo = pad_lo_ref[ec]
        return lo, jnp.where(e < N_EXPERTS, pad_hi_ref[ec], lo)

    def pad_chunks(step, fn):
        for q in range(experts_per_step):
            lo, hi = pad_range(step, q)
            count = hi - lo
            p = EXPERT_ROWS // 2
            while p >= 1:
                @pl.when((count & p) != 0)
                def _(p=p):
                    start = hi - (count & ~(2 * p - 1)) - p
                    fn(pltpu.make_async_copy(
                        zero_ref.at[pl.ds(0, p * ROW_TILE)],
                        xpad_ref.at[pl.ds(pl.multiple_of(start * ROW_TILE, p * ROW_TILE), p * ROW_TILE)],
                        pad_sem))
                p //= 2

    def start_pad(step):
        pad_chunks(step, lambda copy: copy.start())

    def wait_pad(step):
        pad_chunks(step, lambda copy: copy.wait())

    def tail_copy(tile):
        rows = EXPERT_ROWS * ROW_TILE
        return pltpu.make_async_copy(
            zero_ref, xpad_ref.at[pl.ds(pl.multiple_of(tile * rows, rows), rows)], tail_sem)

    def start_tail(tile, c):
        tail_copy(tile).start()
        return c

    def wait_tail(tile, c):
        tail_copy(tile).wait()
        return c

    n_tiles = xpad_ref.shape[0] // (EXPERT_ROWS * ROW_TILE)

    @pl.when(i == 0)
    def _():
        zero_ref[...] = jnp.zeros(zero_ref.shape, zero_ref.dtype)
        lax.fori_loop(nvalid_ref[0], n_tiles, start_tail, 0)

    start_pad(i)

    stage_ref[slot] = ht_ref[...]

    def issue(r, carry):
        src = _row_tile_at(stage_ref.at[slot], r)
        for kk in range(TOP_K):
            pltpu.make_async_copy(
                src, _row_tile_at(xpad_ref, dest_ref[r * TOP_K + kk]), sem.at[slot]).start(priority=kk % 2)
        return carry

    lax.fori_loop(0, tm, issue, 0)

    def drain(s):
        for _kk in range(TOP_K):
            pltpu.make_async_copy(
                stage_ref.at[s], xpad_ref.at[pl.ds(0, tm * ROW_TILE)], sem.at[s]).wait()

    @pl.when(i > 0)
    def _():
        drain(1 - slot)
        wait_pad(i - 1)

    @pl.when(i == n - 1)
    def _():
        drain(slot)
        wait_pad(i)
        lax.fori_loop(nvalid_ref[0], n_tiles, wait_tail, 0)


def _dispatch(dest_flat, pad_lo, pad_hi, n_valid, ht, cap):
    t = ht.shape[0] // ROW_TILE
    tm = DISPATCH_ROWS
    experts_per_step = -(-N_EXPERTS // (t // tm))
    return pl.pallas_call(
        functools.partial(_dispatch_kernel, experts_per_step=experts_per_step),
        name="dispatch",
        grid=(t // tm,),
        in_specs=[
            pl.BlockSpec((tm * TOP_K,), lambda i: (i,), memory_space=pltpu.SMEM),
            pl.BlockSpec(memory_space=pltpu.SMEM),
            pl.BlockSpec(memory_space=pltpu.SMEM),
            pl.BlockSpec(memory_space=pltpu.SMEM),
            pl.BlockSpec((tm * ROW_TILE, LANES), lambda i: (i, 0)),
        ],
        out_specs=pl.BlockSpec(memory_space=pl.ANY),
        out_shape=jax.ShapeDtypeStruct((cap * ROW_TILE, LANES), ht.dtype),
        scratch_shapes=[
            pltpu.VMEM((2, tm * ROW_TILE, LANES), ht.dtype),
            pltpu.VMEM((EXPERT_ROWS * ROW_TILE, LANES), ht.dtype),
            pltpu.SemaphoreType.DMA((2,)),
            pltpu.SemaphoreType.DMA(()),
            pltpu.SemaphoreType.DMA(()),
        ],
        compiler_params=pltpu.CompilerParams(
            dimension_semantics=("arbitrary",), has_side_effects=True),
    )(dest_flat, pad_lo, pad_hi, n_valid, ht)


def _expert_kernel(texp_ref, nvalid_ref, first_ref, next_ref, par_ref,
                   x_hbm, w1_hbm, w3_hbm, w2_hbm, y_ref,
                   w1f_ref, w3f_ref, w2f_ref, w1b_ref, w3b_ref, w2b_ref, hid_ref, xbuf_ref, sem, xsem):
    i = pl.program_id(0)
    n_tiles = pl.num_programs(0) - 1
    tm = EXPERT_ROWS
    cur = jnp.minimum(i, n_tiles - 1)
    prev = jnp.maximum(i - 1, 0)
    cur_valid = i < nvalid_ref[0]
    prev_valid = jnp.logical_and(i >= 1, i - 1 < nvalid_ref[0])

    def x_copy(tile):
        rows = tm * ROW_TILE
        return pltpu.make_async_copy(
            x_hbm.at[pl.ds(pl.multiple_of(tile * rows, rows), rows)],
            xbuf_ref.at[tile % X_BUFFERS], xsem.at[tile % X_BUFFERS])

    @pl.when(i == 0)
    def _():
        for t in range(X_BUFFERS - 1):
            @pl.when(t < nvalid_ref[0])
            def _(t=t):
                x_copy(t).start()

    @pl.when(i + (X_BUFFERS - 1) < nvalid_ref[0])
    def _():
        x_copy(i + (X_BUFFERS - 1)).start()

    @pl.when(cur_valid)
    def _():
        x_copy(i).wait()

    def weight_copies(e, s):
        return (pltpu.make_async_copy(w1_hbm.at[e], w1f_ref.at[s], sem.at[s]),
                pltpu.make_async_copy(w3_hbm.at[e], w3f_ref.at[s], sem.at[s]),
                pltpu.make_async_copy(w2_hbm.at[e], w2f_ref.at[s], sem.at[s]))

    @pl.when(jnp.logical_and(cur_valid, first_ref[cur] == 1))
    def _():
        s = par_ref[cur]

        @pl.when(i == 0)
        def _():
            for c in weight_copies(texp_ref[0], 0):
                c.start()

        for c in weight_copies(texp_ref[cur], s):
            c.wait()
        nxt = next_ref[cur]

        @pl.when(nxt >= 0)
        def _():
            for c in weight_copies(nxt, 1 - s):
                c.start()

        w1b_ref[...] = w1f_ref[s].astype(BF16)
        w3b_ref[...] = w3f_ref[s].astype(BF16)
        w2b_ref[s] = w2f_ref[s].astype(BF16)

    def up_matmuls():
        chunks = _load_packed_chunks(xbuf_ref.at[i % X_BUFFERS], tm)
        xb = jnp.concatenate([c.astype(BF16) for c in chunks], axis=1)
        return _dot(xb, w1b_ref[...]), _dot(xb, w3b_ref[...])

    def down_matmul():
        return _dot(hid_ref[(i - 1) % 2], w2b_ref[par_ref[prev]])

    def store_hidden(g, u):
        hid_ref[i % 2] = (g * _sigmoid(g) * u).astype(BF16)

    @pl.when(jnp.logical_and(cur_valid, prev_valid))
    def _():
        g, u = up_matmuls()
        y = down_matmul()
        store_hidden(g, u)
        _store_packed_rows(y_ref, y)

    @pl.when(jnp.logical_and(cur_valid, jnp.logical_not(prev_valid)))
    def _():
        store_hidden(*up_matmuls())

    @pl.when(jnp.logical_and(jnp.logical_not(cur_valid), prev_valid))
    def _():
        _store_packed_rows(y_ref, down_matmul())

    @pl.when(jnp.logical_and(i >= 1, jnp.logical_not(prev_valid)))
    def _():
        y_ref[...] = jnp.zeros(y_ref.shape, y_ref.dtype)


def _experts(tile_expert, n_valid, tile_first, tile_next, tile_par, xpad, w1, w3, w2):
    tm = EXPERT_ROWS
    n_tiles = xpad.shape[0] // (tm * ROW_TILE)

    grid_spec = pltpu.PrefetchScalarGridSpec(
        num_scalar_prefetch=5,
        grid=(n_tiles + 1,),
        in_specs=[
            pl.BlockSpec(memory_space=pl.ANY),
            pl.BlockSpec(memory_space=pl.ANY),
            pl.BlockSpec(memory_space=pl.ANY),
            pl.BlockSpec(memory_space=pl.ANY),
        ],
        out_specs=pl.BlockSpec((tm * ROW_TILE, LANES), lambda i, te, nv, fi, nx, pa: (jnp.maximum(i - 1, 0), 0)),
        scratch_shapes=[
            pltpu.VMEM((2, D_MODEL, EXPERT_DIM), F32),
            pltpu.VMEM((2, D_MODEL, EXPERT_DIM), F32),
            pltpu.VMEM((2, EXPERT_DIM, D_MODEL), F32),
            pltpu.VMEM((D_MODEL, EXPERT_DIM), BF16),
            pltpu.VMEM((D_MODEL, EXPERT_DIM), BF16),
            pltpu.VMEM((2, EXPERT_DIM, D_MODEL), BF16),
            pltpu.VMEM((2, tm, EXPERT_DIM), BF16),
            pltpu.VMEM((X_BUFFERS, tm * ROW_TILE, LANES), xpad.dtype),
            pltpu.SemaphoreType.DMA((2,)),
            pltpu.SemaphoreType.DMA((X_BUFFERS,)),
        ],
    )
    return pl.pallas_call(
        _expert_kernel,
        name="experts",
        grid_spec=grid_spec,
        out_shape=jax.ShapeDtypeStruct(xpad.shape, xpad.dtype),
        compiler_params=pltpu.CompilerParams(
            dimension_semantics=("arbitrary",), vmem_limit_bytes=VMEM_LIMIT_BYTES),
    )(tile_expert, n_valid, tile_first, tile_next, tile_par, xpad, w1, w3, w2)


def _combine_kernel(dest_ref, dest_next_ref, wt_ref, base_ref, g2_ref, b2_ref, ypad_ref, out_ref, buf_ref, sem):
    i = pl.program_id(0)
    n = pl.num_programs(0)
    tm = COMBINE_ROWS
    slot = i % 2

    def issue_from(d_ref, s):
        def issue(r, carry):
            for kk in range(TOP_K):
                pltpu.make_async_copy(
                    _row_tile_at(ypad_ref, d_ref[r * TOP_K + kk]), _row_tile_at(buf_ref.at[s, kk], r),
                    sem.at[s]).start(priority=kk % 2)
            return carry
        lax.fori_loop(0, tm, issue, 0)

    @pl.when(i == 0)
    def _():
        issue_from(dest_ref, 0)

    @pl.when(i + 1 < n)
    def _():
        issue_from(dest_next_ref, 1 - slot)

    for kk in range(TOP_K):
        pltpu.make_async_copy(
            ypad_ref.at[pl.ds(0, tm * ROW_TILE)], buf_ref.at[slot, kk], sem.at[slot]).wait()

    wt = wt_ref[...]
    n_chunks = D_MODEL // LANES
    parts = [base_ref[:, c * LANES:(c + 1) * LANES] for c in range(n_chunks)]
    for kk in range(TOP_K):
        wk = wt[:, kk:kk + 1]
        chunks = _load_packed_chunks(buf_ref.at[slot, kk], tm)
        parts = [acc + wk * c for acc, c in zip(parts, chunks)]
    r = jnp.concatenate(parts, axis=1)
    out_ref[...] = _layer_norm(r, g2_ref[...], b2_ref[...])


def _combine(dest_flat, wt_tok, base, g2, b2, ypad):
    t = base.shape[0]
    tm = COMBINE_ROWS
    n = t // tm
    rows = lambda i: (i, 0)
    const = lambda i: (0, 0)
    return pl.pallas_call(
        _combine_kernel,
        name="combine",
        grid=(n,),
        in_specs=[
            pl.BlockSpec((tm * TOP_K,), lambda i: (i,), memory_space=pltpu.SMEM),
            pl.BlockSpec((tm * TOP_K,), lambda i: (jnp.minimum(i + 1, n - 1),), memory_space=pltpu.SMEM),
            pl.BlockSpec((tm, TOP_K), rows),
            pl.BlockSpec((tm, D_MODEL), rows),
            pl.BlockSpec(g2.shape, const),
            pl.BlockSpec(b2.shape, const),
            pl.BlockSpec(memory_space=pl.ANY),
        ],
        out_specs=pl.BlockSpec((tm, D_MODEL), rows),
        out_shape=jax.ShapeDtypeStruct((t, D_MODEL), F32),
        scratch_shapes=[
            pltpu.VMEM((2, TOP_K, tm * ROW_TILE, LANES), ypad.dtype),
            pltpu.SemaphoreType.DMA((2,)),
        ],
        compiler_params=pltpu.CompilerParams(
            dimension_semantics=("arbitrary",), vmem_limit_bytes=VMEM_LIMIT_BYTES),
    )(dest_flat, dest_flat, wt_tok, base, g2, b2, ypad)


def _rope_tables(seq):
    half = ROPE_DIM // 2
    inv = jnp.power(ROPE_THETA, -jnp.arange(half, dtype=F32) * 2.0 / ROPE_DIM)
    ang = jnp.arange(seq, dtype=F32)[:, None] * inv[None, :]
    cos = jnp.cos(ang)
    sin = jnp.sin(ang)
    ones = jnp.ones((seq, HEAD_DIM - ROPE_DIM), F32)
    zeros_rest = jnp.zeros((seq, HEAD_DIM - ROPE_DIM), F32)
    zeros_half = jnp.zeros((seq, half), F32)
    cos_h = jnp.concatenate([cos, cos, ones], axis=1)
    sinlo_h = jnp.concatenate([-sin, zeros_half, zeros_rest], axis=1)
    sinhi_h = jnp.concatenate([zeros_half, sin, zeros_rest], axis=1)
    tile = lambda m: jnp.tile(m, (1, N_HEADS))
    return tile(cos_h), tile(sinlo_h), tile(sinhi_h)


def kernel(x, w_in, b_gate, w_conv, w_conv_out, w_attn_out, w_mix_out, ln1_g, ln1_b, w_router, router_bias,
           w1, w3, w2, ws1, ws3, ws2, ln2_g, ln2_b):
    batch, seq, d = x.shape
    t = batch * seq
    assert d == D_MODEL and seq % PROJ_ROWS == 0 and seq % MOBA_BLOCK == 0
    x2 = x.reshape(t, d)
    h = None
    for layer in range(DEPTH):
        c3 = 3 * CONV_WIDTH
        a3 = 3 * ATTN_WIDTH
        w_in_b = w_in[layer].astype(BF16)
        wc = w_in_b[:, :c3]
        wqkv = w_in_b[:, c3:c3 + a3]
        wg = w_in_b[:, c3 + a3:]
        cos, sinlo, sinhi = _rope_tables(seq)
        a, gattn, q, k, v, kmean = _proj(
            x2, wc, wqkv, wg, b_gate[layer][None, :], w_conv[layer], w_conv_out[layer].astype(BF16),
            cos, sinlo, sinhi, seq)
        o = _attention(q, k, v, kmean, batch, seq)
        base, ht, logits_t = _post(
            o, a, gattn, x2, w_attn_out[layer].astype(BF16), w_mix_out[layer].astype(BF16),
            ln1_g[layer][None, :], ln1_b[layer][None, :],
            ws1[layer].astype(BF16), ws3[layer].astype(BF16), ws2[layer].astype(BF16),
            w_router[layer].T)

        tri = (jnp.arange(ROUTE_COLS)[:, None] < jnp.arange(ROUTE_COLS)[None, :]).astype(BF16)
        eid, pos, wts, counts = _route(logits_t, router_bias[layer][:, None], tri)

        cnt = counts[:, 0].astype(I32)
        padded = (cnt + EXPERT_ROWS - 1) // EXPERT_ROWS * EXPERT_ROWS
        pend = jnp.cumsum(padded)
        pstart = pend - padded
        cap = t * TOP_K + N_EXPERTS * EXPERT_ROWS
        n_tiles = cap // EXPERT_ROWS
        dest = _slots(eid, pos, pstart.astype(F32)[:, None])
        dest_flat = dest.T.reshape(t * TOP_K)
        tile_start = jnp.arange(n_tiles, dtype=I32) * EXPERT_ROWS
        tile_expert = jnp.minimum(
            jnp.sum((pend[None, :] <= tile_start[:, None]).astype(I32), axis=1), N_EXPERTS - 1)
        n_valid = (pend[-1:] // EXPERT_ROWS).astype(I32)
        tile_i = jnp.arange(n_tiles, dtype=I32)
        prev_expert = jnp.concatenate([jnp.full((1,), -1, I32), tile_expert[:-1]])
        tile_first = ((tile_i < n_valid[0]) & (tile_expert != prev_expert)).astype(I32)
        first_pos = jnp.where(tile_first == 1, tile_i, n_tiles)
        later_first = lax.cummin(jnp.concatenate([first_pos[1:], jnp.full((1,), n_tiles, I32)]), reverse=True)
        tile_next = jnp.where(later_first < n_tiles, tile_expert[jnp.minimum(later_first, n_tiles - 1)], -1)
        tile_par = (jnp.cumsum(tile_first) - 1) % 2

        xpad = _dispatch(dest_flat, pstart + cnt, pend, n_valid, ht, cap)
        ypad = _experts(tile_expert, n_valid, tile_first, tile_next.astype(I32), tile_par.astype(I32),
                        xpad, w1[layer], w3[layer], w2[layer])
        h = _combine(dest_flat, wts.T, base, ln2_g[layer][None, :], ln2_b[layer][None, :], ypad)
        x2 = h
    return h.reshape(batch, seq, d)
```

```python
import functools

import jax
import jax.numpy as jnp
from jax import lax
from jax.experimental import pallas as pl
from jax.experimental.pallas import tpu as pltpu

D_MODEL = 1024
CONV_WIDTH = 1024
CONV_K = 3
N_HEADS = 8
HEAD_DIM = 64
ATTN_WIDTH = N_HEADS * HEAD_DIM
MOBA_BLOCK = 256
MOBA_TOPK = 3
ROPE_THETA = 500000.0
ROPE_DIM = HEAD_DIM // 4
N_EXPERTS = 256
TOP_K = 8
N_GROUPS = 8
TOPK_GROUPS = 4
EXPERT_DIM = 256
SHARED_DIM = 256
ROUTED_SCALE = 2.5
DEPTH = 1
DN_ALPHA = (2 * DEPTH) ** 0.25
LN_EPS = 1e-5

NEG_BIG = -1e30
QK_SCALE = HEAD_DIM ** -0.5
assert QK_SCALE == 2.0 ** -3

VMEM_LIMIT_BYTES = 56 * 1024 * 1024

PROJ_ROWS = 512
ROUTE_COLS = 256
SLOT_COLS = 1024
EXPERT_ROWS = 256
X_BUFFERS = 4
DISPATCH_ROWS = 256
COMBINE_ROWS = 256

F32 = jnp.float32
BF16 = jnp.bfloat16
U32 = jnp.uint32
I32 = jnp.int32


def _sigmoid(v):
    return 1.0 / (1.0 + jnp.exp(-v))


def _dot(a, b):
    return jnp.dot(a, b, preferred_element_type=F32)


def _dot_nt(a, b):
    return lax.dot_general(a, b, (((1,), (1,)), ((), ())), preferred_element_type=F32)


ROW_TILE = 4
LANES = 128
HALF = ROW_TILE * LANES


def _store_packed_rows(ref, val):
    m = val.shape[0]
    for i in range(ROW_TILE):
        lo = val[:, i * LANES:(i + 1) * LANES]
        hi = val[:, HALF + i * LANES:HALF + (i + 1) * LANES]
        ref[pl.ds(i, m, stride=ROW_TILE), :] = pltpu.pack_elementwise([lo, hi], packed_dtype=BF16)


def _load_packed_chunks(ref, m):
    words = [ref[pl.ds(i, m, stride=ROW_TILE), :] for i in range(ROW_TILE)]
    unpack = lambda w, idx: pltpu.unpack_elementwise(w, index=idx, packed_dtype=BF16, unpacked_dtype=F32)
    return [unpack(w, 0) for w in words] + [unpack(w, 1) for w in words]


def _layer_norm(r, g, b):
    mu = jnp.mean(r, axis=-1, keepdims=True)
    c = r - mu
    var = jnp.mean(c * c, axis=-1, keepdims=True)
    return c * lax.rsqrt(var + LN_EPS) * g + b


def _proj_kernel(x_ref, wc_ref, wqkv_ref, wg_ref, bg_ref, wconv_ref, wco_ref,
                 cos_ref, sinlo_ref, sinhi_ref,
                 a_ref, gattn_ref, q_ref, k_ref, vt_ref, kmean_ref,
                 halo_ref, *, tiles_per_seq):
    tm = x_ref.shape[0]
    i = pl.program_id(0)
    xb = x_ref[...].astype(BF16)

    cb = _dot(xb, wc_ref[:, 0:CONV_WIDTH])
    cc = _dot(xb, wc_ref[:, CONV_WIDTH:2 * CONV_WIDTH])
    cv = _dot(xb, wc_ref[:, 2 * CONV_WIDTH:3 * CONV_WIDTH])
    u = cc * cv
    @pl.when((i % tiles_per_seq) == 0)
    def _():
        halo_ref[...] = jnp.zeros(halo_ref.shape, F32)

    prev = halo_ref[...]
    row = lax.broadcasted_iota(I32, u.shape, 0)
    u1 = jnp.where(row == 0, prev[7:8, :], pltpu.roll(u, 1, 0))
    u2 = jnp.where(row == 0, prev[6:7, :], jnp.where(row == 1, prev[7:8, :], pltpu.roll(u, 2, 0)))
    halo_ref[...] = u[tm - 8:tm, :]
    w0 = wconv_ref[0:1, :]
    w1 = wconv_ref[1:2, :]
    w2 = wconv_ref[2:3, :]
    yc = cb * (w0 * u2 + w1 * u1 + w2 * u)
    y_conv = _dot(yc.astype(BF16), wco_ref[...])

    g = _dot(xb, wg_ref[...]) + bg_ref[...]
    gates = _sigmoid(g)
    a_ref[...] = (gates[:, :D_MODEL] * y_conv).astype(a_ref.dtype)
    gattn_ref[...] = gates[:, D_MODEL:].astype(gattn_ref.dtype)

    qkv = _dot(xb, wqkv_ref[...])
    cos = cos_ref[...]
    sinlo = sinlo_ref[...]
    sinhi = sinhi_ref[...]

    def rope(t):
        return (t * cos + pltpu.roll(t, ATTN_WIDTH - ROPE_DIM // 2, 1) * sinlo
                + pltpu.roll(t, ROPE_DIM // 2, 1) * sinhi)

    q = rope(qkv[:, 0:ATTN_WIDTH])
    k = rope(qkv[:, ATTN_WIDTH:2 * ATTN_WIDTH])
    q_ref[...] = (q * QK_SCALE).astype(q_ref.dtype)
    k_ref[...] = k.astype(k_ref.dtype)
    v = qkv[:, 2 * ATTN_WIDTH:]
    for blk in range(tm // MOBA_BLOCK):
        rows = slice(blk * MOBA_BLOCK, (blk + 1) * MOBA_BLOCK)
        kmean_ref[blk] = jnp.mean(k[rows, :], axis=0, keepdims=True)
        vt_ref[blk] = v[rows, :].T.astype(vt_ref.dtype)


def _proj(x2, wc, wqkv, wg, bg, wconv, wco, cos, sinlo, sinhi, seq):
    t = x2.shape[0]
    tm = PROJ_ROWS
    tiles_per_seq = seq // tm
    const = lambda i: (0, 0)
    rows = lambda i: (i, 0)
    pos = lambda i: (i % tiles_per_seq, 0)
    return pl.pallas_call(
        functools.partial(_proj_kernel, tiles_per_seq=tiles_per_seq),
        name="proj",
        grid=(t // tm,),
        in_specs=[
            pl.BlockSpec((tm, D_MODEL), rows),
            pl.BlockSpec(wc.shape, const),
            pl.BlockSpec(wqkv.shape, const),
            pl.BlockSpec(wg.shape, const),
            pl.BlockSpec(bg.shape, const),
            pl.BlockSpec(wconv.shape, const),
            pl.BlockSpec(wco.shape, const),
            pl.BlockSpec((tm, ATTN_WIDTH), pos),
            pl.BlockSpec((tm, ATTN_WIDTH), pos),
            pl.BlockSpec((tm, ATTN_WIDTH), pos),
        ],
        out_specs=[
            pl.BlockSpec((tm, D_MODEL), rows),
            pl.BlockSpec((tm, D_MODEL), rows),
            pl.BlockSpec((tm, ATTN_WIDTH), rows),
            pl.BlockSpec((tm, ATTN_WIDTH), rows),
            pl.BlockSpec((tm // MOBA_BLOCK, ATTN_WIDTH, MOBA_BLOCK), lambda i: (i, 0, 0)),
            pl.BlockSpec((tm // MOBA_BLOCK, 1, ATTN_WIDTH), lambda i: (i, 0, 0)),
        ],
        out_shape=[
            jax.ShapeDtypeStruct((t, D_MODEL), BF16),
            jax.ShapeDtypeStruct((t, D_MODEL), BF16),
            jax.ShapeDtypeStruct((t, ATTN_WIDTH), BF16),
            jax.ShapeDtypeStruct((t, ATTN_WIDTH), BF16),
            jax.ShapeDtypeStruct((t // MOBA_BLOCK, ATTN_WIDTH, MOBA_BLOCK), BF16),
            jax.ShapeDtypeStruct((t // MOBA_BLOCK, 1, ATTN_WIDTH), F32),
        ],
        scratch_shapes=[pltpu.VMEM((8, CONV_WIDTH), F32)],
        compiler_params=pltpu.CompilerParams(
            dimension_semantics=("arbitrary",), vmem_limit_bytes=VMEM_LIMIT_BYTES),
    )(x2, wc, wqkv, wg, bg, wconv, wco, cos, sinlo, sinhi)


def _attn_kernel(q_ref, k_ref, vt_ref, kmean_ref, o_ref, bias_ref, m_ref, l_ref, alpha_ref, p_ref, acc_ref,
                 *, n_blocks):
    j = pl.program_id(1)
    blk_rows = MOBA_BLOCK
    key_i = lax.broadcasted_iota(I32, (blk_rows, blk_rows), 0)
    qry_i = lax.broadcasted_iota(I32, (blk_rows, blk_rows), 1)
    causal = key_i <= qry_i
    blk_i = lax.broadcasted_iota(I32, (n_blocks, blk_rows), 0)
    past = blk_i < j

    for h in range(N_HEADS):
        cols = slice(h * HEAD_DIM, (h + 1) * HEAD_DIM)
        qh = q_ref[:, cols]
        km = kmean_ref[:, 0, cols]
        gate = _dot_nt(km, qh.astype(F32))
        gate = jnp.where(past, gate, -jnp.inf)
        rank = jnp.zeros(gate.shape, F32)
        for m in range(n_blocks):
            gm = gate[m:m + 1, :]
            ahead = jnp.where(gm > gate, 1.0, jnp.where(gm == gate, jnp.where(blk_i > m, 1.0, 0.0), 0.0))
            rank = rank + ahead
        bias_ref[h] = jnp.where(past, jnp.where(rank < MOBA_TOPK, 0.0, NEG_BIG), NEG_BIG)

    def scores(n):
        r = pl.multiple_of(n * blk_rows, blk_rows)
        return [_dot_nt(k_ref[pl.ds(r, blk_rows), h * HEAD_DIM:(h + 1) * HEAD_DIM],
                        q_ref[:, h * HEAD_DIM:(h + 1) * HEAD_DIM]) for h in range(N_HEADS)]

    for h, s_raw in enumerate(scores(j)):
        s = jnp.where(causal, s_raw, NEG_BIG)
        m0 = jnp.max(s, axis=0, keepdims=True)
        p = jnp.exp(s - m0)
        m_ref[h:h + 1, :] = m0
        l_ref[h:h + 1, :] = jnp.sum(p, axis=0, keepdims=True)
        p_ref[h] = p.astype(p_ref.dtype)
    for h in range(N_HEADS):
        cols = slice(h * HEAD_DIM, (h + 1) * HEAD_DIM)
        acc_ref[cols, :] = _dot(vt_ref[j, cols, :], p_ref[h])

    def body(n, carry):
        for h, s_raw in enumerate(scores(n)):
            sn = s_raw + bias_ref[h, pl.ds(n, 1), :]
            m_run = m_ref[h:h + 1, :]
            m_new = jnp.maximum(m_run, jnp.max(sn, axis=0, keepdims=True))
            alpha = jnp.exp(m_run - m_new)
            pn = jnp.exp(sn - m_new)
            m_ref[h:h + 1, :] = m_new
            l_ref[h:h + 1, :] = alpha * l_ref[h:h + 1, :] + jnp.sum(pn, axis=0, keepdims=True)
            alpha_ref[h:h + 1, :] = alpha
            p_ref[h] = pn.astype(p_ref.dtype)
        pvs = [_dot(vt_ref[n, h * HEAD_DIM:(h + 1) * HEAD_DIM, :], p_ref[h]) for h in range(N_HEADS)]
        for h in range(N_HEADS):
            cols = slice(h * HEAD_DIM, (h + 1) * HEAD_DIM)
            acc_ref[cols, :] = acc_ref[cols, :] * alpha_ref[h:h + 1, :] + pvs[h]
        return carry

    lax.fori_loop(0, j, body, 0)
    for h in range(N_HEADS):
        cols = slice(h * HEAD_DIM, (h + 1) * HEAD_DIM)
        acc_ref[cols, :] = acc_ref[cols, :] / l_ref[h:h + 1, :]
    o_ref[...] = acc_ref[...].T.astype(o_ref.dtype)


def _attention(q, k, vt, kmean, batch, seq):
    n_blocks = seq // MOBA_BLOCK
    return pl.pallas_call(
        functools.partial(_attn_kernel, n_blocks=n_blocks),
        name="attention",
        grid=(batch, n_blocks),
        in_specs=[
            pl.BlockSpec((MOBA_BLOCK, ATTN_WIDTH), lambda b, j: (b * n_blocks + j, 0)),
            pl.BlockSpec((seq, ATTN_WIDTH), lambda b, j: (b, 0)),
            pl.BlockSpec((n_blocks, ATTN_WIDTH, MOBA_BLOCK), lambda b, j: (b, 0, 0)),
            pl.BlockSpec((n_blocks, 1, ATTN_WIDTH), lambda b, j: (b, 0, 0)),
        ],
        out_specs=pl.BlockSpec((MOBA_BLOCK, ATTN_WIDTH), lambda b, j: (b * n_blocks + j, 0)),
        out_shape=jax.ShapeDtypeStruct(q.shape, BF16),
        scratch_shapes=[
            pltpu.VMEM((N_HEADS, n_blocks, MOBA_BLOCK), F32),
            pltpu.VMEM((N_HEADS, MOBA_BLOCK), F32),
            pltpu.VMEM((N_HEADS, MOBA_BLOCK), F32),
            pltpu.VMEM((N_HEADS, MOBA_BLOCK), F32),
            pltpu.VMEM((N_HEADS, MOBA_BLOCK, MOBA_BLOCK), BF16),
            pltpu.VMEM((ATTN_WIDTH, MOBA_BLOCK), F32),
        ],
        compiler_params=pltpu.CompilerParams(
            dimension_semantics=("arbitrary", "arbitrary"), vmem_limit_bytes=VMEM_LIMIT_BYTES),
    )(q, k, vt, kmean)


def _post_kernel(o_ref, a_ref, gattn_ref, x_ref, wao_ref, wmo_ref, g1_ref, b1_ref,
                 ws1_ref, ws3_ref, ws2_ref, wrt_ref, rbias_ref, tri_ref,
                 base_ref, ht_ref, eid_ref, pos_ref, wt_ref, cnt_ref, run_ref, logit_ref):
    i = pl.program_id(0)
    n = pl.num_programs(0) - 1

    def route_previous():
        _route_tile(logit_ref[(i - 1) % 2], rbias_ref, tri_ref, eid_ref, pos_ref, wt_ref, cnt_ref, run_ref)

    def main():
        _post_tile(o_ref, a_ref, gattn_ref, x_ref, wao_ref, wmo_ref, g1_ref, b1_ref,
                   ws1_ref, ws3_ref, ws2_ref, wrt_ref, base_ref, ht_ref, logit_ref.at[i % 2])

    @pl.when(i == 0)
    def _():
        run_ref[...] = jnp.zeros(run_ref.shape, F32)
        main()

    @pl.when(jnp.logical_and(i >= 1, i < n))
    def _():
        main()
        route_previous()

    @pl.when(i == n)
    def _():
        route_previous()


def _post_tile(o_ref, a_ref, gattn_ref, x_ref, wao_ref, wmo_ref, g1_ref, b1_ref,
               ws1_ref, ws3_ref, ws2_ref, wrt_ref, base_ref, ht_ref, logit_out_ref):
    y_attn = _dot(o_ref[...], wao_ref[...])
    merged = a_ref[...].astype(F32) + gattn_ref[...].astype(F32) * y_attn
    mix = _dot(merged.astype(BF16), wmo_ref[...])
    h = _layer_norm(DN_ALPHA * x_ref[...] + mix, g1_ref[...], b1_ref[...])
    hb = h.astype(BF16)
    s1 = _dot(hb, ws1_ref[...])
    s3 = _dot(hb, ws3_ref[...])
    shared = _dot((s1 * _sigmoid(s1) * s3).astype(BF16), ws2_ref[...])
    base_ref[...] = DN_ALPHA * h + shared
    _store_packed_rows(ht_ref, h)
    logit_out_ref[...] = _dot_nt(wrt_ref[...], h)


def _post(o, a, gattn, x2, wao, wmo, g1, b1, ws1, ws3, ws2, wrt, rbias_col, tri):
    t = x2.shape[0]
    tm = PROJ_ROWS
    n = t // tm
    const = lambda i: (0, 0)
    rows = lambda i: (jnp.minimum(i, n - 1), 0)
    cols = lambda i: (0, jnp.maximum(i - 1, 0))
    return pl.pallas_call(
        _post_kernel,
        name="post",
        grid=(n + 1,),
        in_specs=[
            pl.BlockSpec((tm, ATTN_WIDTH), rows),
            pl.BlockSpec((tm, D_MODEL), rows),
            pl.BlockSpec((tm, D_MODEL), rows),
            pl.BlockSpec((tm, D_MODEL), rows),
            pl.BlockSpec(wao.shape, const),
            pl.BlockSpec(wmo.shape, const),
            pl.BlockSpec(g1.shape, const),
            pl.BlockSpec(b1.shape, const),
            pl.BlockSpec(ws1.shape, const),
            pl.BlockSpec(ws3.shape, const),
            pl.BlockSpec(ws2.shape, const),
            pl.BlockSpec(wrt.shape, const),
            pl.BlockSpec((N_EXPERTS, 1), const),
            pl.BlockSpec((tm, tm), const),
        ],
        out_specs=[
            pl.BlockSpec((tm, D_MODEL), rows),
            pl.BlockSpec((tm * ROW_TILE, LANES), rows),
            pl.BlockSpec((TOP_K, tm), cols),
            pl.BlockSpec((TOP_K, tm), cols),
            pl.BlockSpec((TOP_K, tm), cols),
            pl.BlockSpec((N_EXPERTS, 1), const),
        ],
        out_shape=[
            jax.ShapeDtypeStruct((t, D_MODEL), F32),
            jax.ShapeDtypeStruct((t * ROW_TILE, LANES), U32),
            jax.ShapeDtypeStruct((TOP_K, t), I32),
            jax.ShapeDtypeStruct((TOP_K, t), I32),
            jax.ShapeDtypeStruct((TOP_K, t), F32),
            jax.ShapeDtypeStruct((N_EXPERTS, 1), F32),
        ],
        scratch_shapes=[pltpu.VMEM((N_EXPERTS, 1), F32), pltpu.VMEM((2, N_EXPERTS, tm), F32)],
        compiler_params=pltpu.CompilerParams(
            dimension_semantics=("arbitrary",), vmem_limit_bytes=VMEM_LIMIT_BYTES),
    )(o, a, gattn, x2, wao, wmo, g1, b1, ws1, ws3, ws2, wrt, rbias_col, tri)


def _route_kernel(logit_ref, bias_ref, tri_ref, eid_ref, pos_ref, wt_ref, cnt_ref, run_ref):
    _route_tile(logit_ref[...], bias_ref, tri_ref, eid_ref, pos_ref, wt_ref, cnt_ref, run_ref)


def _route_tile(logits, bias_ref, tri_ref, eid_ref, pos_ref, wt_ref, cnt_ref, run_ref):
    i = pl.program_id(0)
    tm = logits.shape[1]
    per_group = N_EXPERTS // N_GROUPS

    @pl.when(i == 0)
    def _():
        run_ref[...] = jnp.zeros(run_ref.shape, F32)

    scores = _sigmoid(logits)
    biased = scores + bias_ref[...]
    b3 = biased.reshape(N_GROUPS, per_group, tm)
    i3 = lax.broadcasted_iota(I32, b3.shape, 1)
    m1 = jnp.max(b3, axis=1)
    first = jnp.min(jnp.where(b3 == m1[:, None, :], i3, per_group), axis=1)
    m2 = jnp.max(jnp.where(i3 == first[:, None, :], -jnp.inf, b3), axis=1)
    gscore = m1 + m2
    g_i = lax.broadcasted_iota(I32, gscore.shape, 0)
    grank = jnp.zeros(gscore.shape, F32)
    for m in range(N_GROUPS):
        gm = gscore[m:m + 1, :]
        ahead = jnp.where(gm > gscore, 1.0, jnp.where(gm == gscore, jnp.where(g_i > m, 1.0, 0.0), 0.0))
        grank = grank + ahead
    gsel = grank < TOPK_GROUPS
    work = jnp.where(gsel[:, None, :], b3, -jnp.inf).reshape(N_EXPERTS, tm)

    e_i = lax.broadcasted_iota(I32, (N_EXPERTS, tm), 0)
    sel = jnp.zeros((N_EXPERTS, tm), F32)
    idxs = []
    scs = []
    for _k in range(TOP_K):
        mx = jnp.max(work, axis=0, keepdims=True)
        idx = jnp.min(jnp.where(work == mx, e_i, N_EXPERTS), axis=0, keepdims=True)
        hit = e_i == idx
        scs.append(jnp.sum(jnp.where(hit, scores, 0.0), axis=0, keepdims=True))
        sel = sel + jnp.where(hit, 1.0, 0.0)
        work = jnp.where(hit, -jnp.inf, work)
        idxs.append(idx)

    rank = _dot(sel.astype(BF16), tri_ref[...]) + run_ref[...]
    run_ref[...] = run_ref[...] + jnp.sum(sel, axis=1, keepdims=True)
    cnt_ref[...] = run_ref[...]

    wsum = scs[0]
    for kk in range(1, TOP_K):
        wsum = wsum + scs[kk]
    for kk in range(TOP_K):
        pos = jnp.sum(jnp.where(e_i == idxs[kk], rank, 0.0), axis=0, keepdims=True)
        eid_ref[kk:kk + 1, :] = idxs[kk]
        pos_ref[kk:kk + 1, :] = pos.astype(I32)
        wt_ref[kk:kk + 1, :] = scs[kk] / wsum * ROUTED_SCALE


def _route(logits_t, bias_col, tri):
    t = logits_t.shape[1]
    tm = ROUTE_COLS
    cols = lambda i: (0, i)
    const = lambda i: (0, 0)
    return pl.pallas_call(
        _route_kernel,
        name="route",
        grid=(t // tm,),
        in_specs=[
            pl.BlockSpec((N_EXPERTS, tm), cols),
            pl.BlockSpec((N_EXPERTS, 1), const),
            pl.BlockSpec((tm, tm), const),
        ],
        out_specs=[
            pl.BlockSpec((TOP_K, tm), cols),
            pl.BlockSpec((TOP_K, tm), cols),
            pl.BlockSpec((TOP_K, tm), cols),
            pl.BlockSpec((N_EXPERTS, 1), const),
        ],
        out_shape=[
            jax.ShapeDtypeStruct((TOP_K, t), I32),
            jax.ShapeDtypeStruct((TOP_K, t), I32),
            jax.ShapeDtypeStruct((TOP_K, t), F32),
            jax.ShapeDtypeStruct((N_EXPERTS, 1), F32),
        ],
        scratch_shapes=[pltpu.VMEM((N_EXPERTS, 1), F32)],
        compiler_params=pltpu.CompilerParams(
            dimension_semantics=("arbitrary",), vmem_limit_bytes=VMEM_LIMIT_BYTES),
    )(logits_t, bias_col, tri)


def _slots_kernel(eid_ref, pos_ref, pstart_ref, dest_ref):
    tm = eid_ref.shape[1]
    e_i = lax.broadcasted_iota(I32, (N_EXPERTS, tm), 0)
    pstart = pstart_ref[...]
    for kk in range(TOP_K):
        start = jnp.sum(jnp.where(e_i == eid_ref[kk:kk + 1, :], pstart, 0.0), axis=0, keepdims=True)
        dest_ref[kk:kk + 1, :] = start.astype(I32) + pos_ref[kk:kk + 1, :]


def _slots(eid, pos, pstart_col):
    t = eid.shape[1]
    tm = SLOT_COLS
    cols = lambda i: (0, i)
    return pl.pallas_call(
        _slots_kernel,
        name="slots",
        grid=(t // tm,),
        in_specs=[
            pl.BlockSpec((TOP_K, tm), cols),
            pl.BlockSpec((TOP_K, tm), cols),
            pl.BlockSpec((N_EXPERTS, 1), lambda i: (0, 0)),
        ],
        out_specs=pl.BlockSpec((TOP_K, tm), cols),
        out_shape=jax.ShapeDtypeStruct((TOP_K, t), I32),
        compiler_params=pltpu.CompilerParams(dimension_semantics=("arbitrary",)),
    )(eid, pos, pstart_col)


def _row_tile_at(ref, row):
    return ref.at[pl.ds(pl.multiple_of(row * ROW_TILE, ROW_TILE), ROW_TILE)]


def _dispatch_kernel(dest_ref, pad_lo_ref, pad_hi_ref, nvalid_ref, ht_ref, xpad_ref,
                     stage_ref, zero_ref, sem, pad_sem, tail_sem, *, experts_per_step):
    i = pl.program_id(0)
    n = pl.num_programs(0)
    tm = DISPATCH_ROWS
    slot = i % 2

    def pad_range(step, q):
        e = step * experts_per_step + q
        ec = jnp.minimum(e, N_EXPERTS - 1)
        lo = pad_lo_ref[ec]
        return lo, jnp.where(e < N_EXPERTS, pad_hi_ref[ec], lo)

    def pad_chunks(step, fn):
        for q in range(experts_per_step):
            lo, hi = pad_range(step, q)
            count = hi - lo
            p = EXPERT_ROWS // 2
            while p >= 1:
                @pl.when((count & p) != 0)
                def _(p=p):
                    start = hi - (count & ~(2 * p - 1)) - p
                    fn(pltpu.make_async_copy(
                        zero_ref.at[pl.ds(0, p * ROW_TILE)],
                        xpad_ref.at[pl.ds(pl.multiple_of(start * ROW_TILE, p * ROW_TILE), p * ROW_TILE)],
                        pad_sem))
                p //= 2

    def start_pad(step):
        pad_chunks(step, lambda copy: copy.start())

    def wait_pad(step):
        pad_chunks(step, lambda copy: copy.wait())

    def tail_copy(tile):
        rows = EXPERT_ROWS * ROW_TILE
        return pltpu.make_async_copy(
            zero_ref, xpad_ref.at[pl.ds(pl.multiple_of(tile * rows, rows), rows)], tail_sem)

    def start_tail(tile, c):
        tail_copy(tile).start()
        return c

    def wait_tail(tile, c):
        tail_copy(tile).wait()
        return c

    n_tiles = xpad_ref.shape[0] // (EXPERT_ROWS * ROW_TILE)

    @pl.when(i == 0)
    def _():
        zero_ref[...] = jnp.zeros(zero_ref.shape, zero_ref.dtype)
        lax.fori_loop(nvalid_ref[0], n_tiles, start_tail, 0)

    start_pad(i)

    stage_ref[slot] = ht_ref[...]

    def issue(r, carry):
        src = _row_tile_at(stage_ref.at[slot], r)
        for kk in range(TOP_K):
            pltpu.make_async_copy(
                src, _row_tile_at(xpad_ref, dest_ref[r * TOP_K + kk]), sem.at[slot]).start(priority=kk % 2)
        return carry

    lax.fori_loop(0, tm, issue, 0)

    def drain(s):
        for _kk in range(TOP_K):
            pltpu.make_async_copy(
                stage_ref.at[s], xpad_ref.at[pl.ds(0, tm * ROW_TILE)], sem.at[s]).wait()

    @pl.when(i > 0)
    def _():
        drain(1 - slot)
        wait_pad(i - 1)

    @pl.when(i == n - 1)
    def _():
        drain(slot)
        wait_pad(i)
        lax.fori_loop(nvalid_ref[0], n_tiles, wait_tail, 0)


def _dispatch(dest_flat, pad_lo, pad_hi, n_valid, ht, cap):
    t = ht.shape[0] // ROW_TILE
    tm = DISPATCH_ROWS
    experts_per_step = -(-N_EXPERTS // (t // tm))
    return pl.pallas_call(
        functools.partial(_dispatch_kernel, experts_per_step=experts_per_step),
        name="dispatch",
        grid=(t // tm,),
        in_specs=[
            pl.BlockSpec((tm * TOP_K,), lambda i: (i,), memory_space=pltpu.SMEM),
            pl.BlockSpec(memory_space=pltpu.SMEM),
            pl.BlockSpec(memory_space=pltpu.SMEM),
            pl.BlockSpec(memory_space=pltpu.SMEM),
            pl.BlockSpec((tm * ROW_TILE, LANES), lambda i: (i, 0)),
        ],
        out_specs=pl.BlockSpec(memory_space=pl.ANY),
        out_shape=jax.ShapeDtypeStruct((cap * ROW_TILE, LANES), ht.dtype),
        scratch_shapes=[
            pltpu.VMEM((2, tm * ROW_TILE, LANES), ht.dtype),
            pltpu.VMEM((EXPERT_ROWS * ROW_TILE, LANES), ht.dtype),
            pltpu.SemaphoreType.DMA((2,)),
            pltpu.SemaphoreType.DMA(()),
            pltpu.SemaphoreType.DMA(()),
        ],
        compiler_params=pltpu.CompilerParams(
            dimension_semantics=("arbitrary",), has_side_effects=True),
    )(dest_flat, pad_lo, pad_hi, n_valid, ht)


def _expert_kernel(texp_ref, nvalid_ref, first_ref, next_ref, par_ref,
                   x_hbm, w1_hbm, w3_hbm, w2_hbm, y_ref,
                   w1f_ref, w3f_ref, w2f_ref, w1b_ref, w3b_ref, w2b_ref, hid_ref, xbuf_ref, sem, xsem):
    i = pl.program_id(0)
    n_tiles = pl.num_programs(0) - 1
    tm = EXPERT_ROWS
    cur = jnp.minimum(i, n_tiles - 1)
    prev = jnp.maximum(i - 1, 0)
    cur_valid = i < nvalid_ref[0]
    prev_valid = jnp.logical_and(i >= 1, i - 1 < nvalid_ref[0])

    def x_copy(tile):
        rows = tm * ROW_TILE
        return pltpu.make_async_copy(
            x_hbm.at[pl.ds(pl.multiple_of(tile * rows, rows), rows)],
            xbuf_ref.at[tile % X_BUFFERS], xsem.at[tile % X_BUFFERS])

    @pl.when(i == 0)
    def _():
        for t in range(X_BUFFERS - 1):
            @pl.when(t < nvalid_ref[0])
            def _(t=t):
                x_copy(t).start()

    @pl.when(i + (X_BUFFERS - 1) < nvalid_ref[0])
    def _():
        x_copy(i + (X_BUFFERS - 1)).start()

    @pl.when(cur_valid)
    def _():
        x_copy(i).wait()

    def weight_copies(e, s):
        return (pltpu.make_async_copy(w1_hbm.at[e], w1f_ref.at[s], sem.at[s]),
                pltpu.make_async_copy(w3_hbm.at[e], w3f_ref.at[s], sem.at[s]),
                pltpu.make_async_copy(w2_hbm.at[e], w2f_ref.at[s], sem.at[s]))

    @pl.when(jnp.logical_and(cur_valid, first_ref[cur] == 1))
    def _():
        s = par_ref[cur]

        @pl.when(i == 0)
        def _():
            for c in weight_copies(texp_ref[0], 0):
                c.start()

        for c in weight_copies(texp_ref[cur], s):
            c.wait()
        nxt = next_ref[cur]

        @pl.when(nxt >= 0)
        def _():
            for c in weight_copies(nxt, 1 - s):
                c.start()

        w1b_ref[...] = w1f_ref[s].astype(BF16)
        w3b_ref[...] = w3f_ref[s].astype(BF16)
        w2b_ref[s] = w2f_ref[s].astype(BF16)

    def up_matmuls():
        chunks = _load_packed_chunks(xbuf_ref.at[i % X_BUFFERS], tm)
        xb = jnp.concatenate([c.astype(BF16) for c in chunks], axis=1)
        return _dot(xb, w1b_ref[...]), _dot(xb, w3b_ref[...])

    def down_matmul():
        return _dot(hid_ref[(i - 1) % 2], w2b_ref[par_ref[prev]])

    def store_hidden(g, u):
        hid_ref[i % 2] = (g * _sigmoid(g) * u).astype(BF16)

    @pl.when(jnp.logical_and(cur_valid, prev_valid))
    def _():
        g, u = up_matmuls()
        y = down_matmul()
        store_hidden(g, u)
        _store_packed_rows(y_ref, y)

    @pl.when(jnp.logical_and(cur_valid, jnp.logical_not(prev_valid)))
    def _():
        store_hidden(*up_matmuls())

    @pl.when(jnp.logical_and(jnp.logical_not(cur_valid), prev_valid))
    def _():
        _store_packed_rows(y_ref, down_matmul())

    @pl.when(jnp.logical_and(i >= 1, jnp.logical_not(prev_valid)))
    def _():
        y_ref[...] = jnp.zeros(y_ref.shape, y_ref.dtype)


def _experts(tile_expert, n_valid, tile_first, tile_next, tile_par, xpad, w1, w3, w2):
    tm = EXPERT_ROWS
    n_tiles = xpad.shape[0] // (tm * ROW_TILE)

    grid_spec = pltpu.PrefetchScalarGridSpec(
        num_scalar_prefetch=5,
        grid=(n_tiles + 1,),
        in_specs=[
            pl.BlockSpec(memory_space=pl.ANY),
            pl.BlockSpec(memory_space=pl.ANY),
            pl.BlockSpec(memory_space=pl.ANY),
            pl.BlockSpec(memory_space=pl.ANY),
        ],
        out_specs=pl.BlockSpec((tm * ROW_TILE, LANES), lambda i, te, nv, fi, nx, pa: (jnp.maximum(i - 1, 0), 0)),
        scratch_shapes=[
            pltpu.VMEM((2, D_MODEL, EXPERT_DIM), F32),
            pltpu.VMEM((2, D_MODEL, EXPERT_DIM), F32),
            pltpu.VMEM((2, EXPERT_DIM, D_MODEL), F32),
            pltpu.VMEM((D_MODEL, EXPERT_DIM), BF16),
            pltpu.VMEM((D_MODEL, EXPERT_DIM), BF16),
            pltpu.VMEM((2, EXPERT_DIM, D_MODEL), BF16),
            pltpu.VMEM((2, tm, EXPERT_DIM), BF16),
            pltpu.VMEM((X_BUFFERS, tm * ROW_TILE, LANES), xpad.dtype),
            pltpu.SemaphoreType.DMA((2,)),
            pltpu.SemaphoreType.DMA((X_BUFFERS,)),
        ],
    )
    return pl.pallas_call(
        _expert_kernel,
        name="experts",
        grid_spec=grid_spec,
        out_shape=jax.ShapeDtypeStruct(xpad.shape, xpad.dtype),
        compiler_params=pltpu.CompilerParams(
            dimension_semantics=("arbitrary",), vmem_limit_bytes=VMEM_LIMIT_BYTES),
    )(tile_expert, n_valid, tile_first, tile_next, tile_par, xpad, w1, w3, w2)


def _combine_kernel(dest_ref, dest_next_ref, wt_ref, base_ref, g2_ref, b2_ref, ypad_ref, out_ref, buf_ref, sem):
    i = pl.program_id(0)
    n = pl.num_programs(0)
    tm = COMBINE_ROWS
    slot = i % 2

    def issue_from(d_ref, s):
        def issue(r, carry):
            for kk in range(TOP_K):
                pltpu.make_async_copy(
                    _row_tile_at(ypad_ref, d_ref[r * TOP_K + kk]), _row_tile_at(buf_ref.at[s, kk], r),
                    sem.at[s]).start(priority=kk % 2)
            return carry
        lax.fori_loop(0, tm, issue, 0)

    @pl.when(i == 0)
    def _():
        issue_from(dest_ref, 0)

    @pl.when(i + 1 < n)
    def _():
        issue_from(dest_next_ref, 1 - slot)

    for kk in range(TOP_K):
        pltpu.make_async_copy(
            ypad_ref.at[pl.ds(0, tm * ROW_TILE)], buf_ref.at[slot, kk], sem.at[slot]).wait()

    wt = wt_ref[...]
    n_chunks = D_MODEL // LANES
    parts = [base_ref[:, c * LANES:(c + 1) * LANES] for c in range(n_chunks)]
    for kk in range(TOP_K):
        wk = wt[:, kk:kk + 1]
        chunks = _load_packed_chunks(buf_ref.at[slot, kk], tm)
        parts = [acc + wk * c for acc, c in zip(parts, chunks)]
    r = jnp.concatenate(parts, axis=1)
    out_ref[...] = _layer_norm(r, g2_ref[...], b2_ref[...])


def _combine(dest_flat, wt_tok, base, g2, b2, ypad):
    t = base.shape[0]
    tm = COMBINE_ROWS
    n = t // tm
    rows = lambda i: (i, 0)
    const = lambda i: (0, 0)
    return pl.pallas_call(
        _combine_kernel,
        name="combine",
        grid=(n,),
        in_specs=[
            pl.BlockSpec((tm * TOP_K,), lambda i: (i,), memory_space=pltpu.SMEM),
            pl.BlockSpec((tm * TOP_K,), lambda i: (jnp.minimum(i + 1, n - 1),), memory_space=pltpu.SMEM),
            pl.BlockSpec((tm, TOP_K), rows),
            pl.BlockSpec((tm, D_MODEL), rows),
            pl.BlockSpec(g2.shape, const),
            pl.BlockSpec(b2.shape, const),
            pl.BlockSpec(memory_space=pl.ANY),
        ],
        out_specs=pl.BlockSpec((tm, D_MODEL), rows),
        out_shape=jax.ShapeDtypeStruct((t, D_MODEL), F32),
        scratch_shapes=[
            pltpu.VMEM((2, TOP_K, tm * ROW_TILE, LANES), ypad.dtype),
            pltpu.SemaphoreType.DMA((2,)),
        ],
        compiler_params=pltpu.CompilerParams(
            dimension_semantics=("arbitrary",), vmem_limit_bytes=VMEM_LIMIT_BYTES),
    )(dest_flat, dest_flat, wt_tok, base, g2, b2, ypad)


def _rope_tables(seq):
    half = ROPE_DIM // 2
    inv = jnp.power(ROPE_THETA, -jnp.arange(half, dtype=F32) * 2.0 / ROPE_DIM)
    ang = jnp.arange(seq, dtype=F32)[:, None] * inv[None, :]
    cos = jnp.cos(ang)
    sin = jnp.sin(ang)
    ones = jnp.ones((seq, HEAD_DIM - ROPE_DIM), F32)
    zeros_rest = jnp.zeros((seq, HEAD_DIM - ROPE_DIM), F32)
    zeros_half = jnp.zeros((seq, half), F32)
    cos_h = jnp.concatenate([cos, cos, ones], axis=1)
    sinlo_h = jnp.concatenate([-sin, zeros_half, zeros_rest], axis=1)
    sinhi_h = jnp.concatenate([zeros_half, sin, zeros_rest], axis=1)
    tile = lambda m: jnp.tile(m, (1, N_HEADS))
    return tile(cos_h), tile(sinlo_h), tile(sinhi_h)


def kernel(x, w_in, b_gate, w_conv, w_conv_out, w_attn_out, w_mix_out, ln1_g, ln1_b, w_router, router_bias,
           w1, w3, w2, ws1, ws3, ws2, ln2_g, ln2_b):
    batch, seq, d = x.shape
    t = batch * seq
    assert d == D_MODEL and seq % PROJ_ROWS == 0 and seq % MOBA_BLOCK == 0
    x2 = x.reshape(t, d)
    h = None
    for layer in range(DEPTH):
        c3 = 3 * CONV_WIDTH
        a3 = 3 * ATTN_WIDTH
        w_in_b = w_in[layer].astype(BF16)
        wc = w_in_b[:, :c3]
        wqkv = w_in_b[:, c3:c3 + a3]
        wg = w_in_b[:, c3 + a3:]
        cos, sinlo, sinhi = _rope_tables(seq)
        a, gattn, q, k, v, kmean = _proj(
            x2, wc, wqkv, wg, b_gate[layer][None, :], w_conv[layer], w_conv_out[layer].astype(BF16),
            cos, sinlo, sinhi, seq)
        o = _attention(q, k, v, kmean, batch, seq)
        tri = (jnp.arange(PROJ_ROWS)[:, None] < jnp.arange(PROJ_ROWS)[None, :]).astype(BF16)
        base, ht, eid, pos, wts, counts = _post(
            o, a, gattn, x2, w_attn_out[layer].astype(BF16), w_mix_out[layer].astype(BF16),
            ln1_g[layer][None, :], ln1_b[layer][None, :],
            ws1[layer].astype(BF16), ws3[layer].astype(BF16), ws2[layer].astype(BF16),
            w_router[layer].T, router_bias[layer][:, None], tri)

        cnt = counts[:, 0].astype(I32)
        padded = (cnt + EXPERT_ROWS - 1) // EXPERT_ROWS * EXPERT_ROWS
        pend = jnp.cumsum(padded)
        pstart = pend - padded
        cap = t * TOP_K + N_EXPERTS * EXPERT_ROWS
        n_tiles = cap // EXPERT_ROWS
        dest = _slots(eid, pos, pstart.astype(F32)[:, None])
        dest_flat = dest.T.reshape(t * TOP_K)
        tile_start = jnp.arange(n_tiles, dtype=I32) * EXPERT_ROWS
        tile_expert = jnp.minimum(
            jnp.sum((pend[None, :] <= tile_start[:, None]).astype(I32), axis=1), N_EXPERTS - 1)
        n_valid = (pend[-1:] // EXPERT_ROWS).astype(I32)
        tile_i = jnp.arange(n_tiles, dtype=I32)
        prev_expert = jnp.concatenate([jnp.full((1,), -1, I32), tile_expert[:-1]])
        tile_first = ((tile_i < n_valid[0]) & (tile_expert != prev_expert)).astype(I32)
        first_pos = jnp.where(tile_first == 1, tile_i, n_tiles)
        later_first = lax.cummin(jnp.concatenate([first_pos[1:], jnp.full((1,), n_tiles, I32)]), reverse=True)
        tile_next = jnp.where(later_first < n_tiles, tile_expert[jnp.minimum(later_first, n_tiles - 1)], -1)
        tile_par = (jnp.cumsum(tile_first) - 1) % 2

        xpad = _dispatch(dest_flat, pstart + cnt, pend, n_valid, ht, cap)
        ypad = _experts(tile_expert, n_valid, tile_first, tile_next.astype(I32), tile_par.astype(I32),
                        xpad, w1[layer], w3[layer], w2[layer])
        h = _combine(dest_flat, wts.T, base, ln2_g[layer][None, :], ln2_b[layer][None, :], ypad)
        x2 = h
    return h.reshape(batch, seq, d)
```
